```python
import jax, jax.numpy as jnp
from jax import lax
import numpy as np

D_MODEL = 1024
BATCH = 8
SEQ = 8192
DEPTH = 2

CHUNK = 64
EPS = 1e-6
HEAD_DV = 64
MIX_W = D_MODEL
RET_W = MIX_W // 4
GLA_W = MIX_W // 4
HG_W = MIX_W - RET_W - GLA_W
RET_HEADS = RET_W // HEAD_DV
HG_HEADS = HG_W // HEAD_DV
GLA_HEADS = GLA_W // HEAD_DV
RET_DK = 64
HG_DK = 64
GLA_DK = HEAD_DV // 2
RET_KW = RET_HEADS * RET_DK
HG_FD = HG_HEADS * HG_DK
GLA_KW = GLA_HEADS * GLA_DK
GLA_RANK = 16
GLA_GATE_NORM = 16.0
ROPE_BASE = 10000.0
D_FF = 256 * ((8 * D_MODEL // 3 + 255) // 256)
CONV_W = 3
IN_SIZES = (RET_KW, RET_KW, RET_W, RET_W,
            HG_FD, HG_FD, HG_W, HG_W,
            GLA_KW, GLA_KW, GLA_W, GLA_W, GLA_RANK)
N_IN = sum(IN_SIZES)

kernel_name = "hybrid_ret_hgrn2_gla_convffn_adaln"


def rms_norm(x):
    xf = x.astype(jnp.float32)
    return (xf * lax.rsqrt(jnp.mean(xf * xf, axis=-1, keepdims=True) + EPS)).astype(x.dtype)


def split_heads(t, h):
    b, s, _ = t.shape
    return t.reshape(b, s, h, -1).transpose(0, 2, 1, 3)


def merge_heads(t):
    b, h, s, d = t.shape
    return t.transpose(0, 2, 1, 3).reshape(b, s, h * d)


def rotary(t, pos):
    d = t.shape[-1]
    inv = ROPE_BASE ** (-jnp.arange(0, d, 2, dtype=jnp.float32) / d)
    ang = pos.astype(jnp.float32)[:, None] * inv[None, :]
    cos, sin = jnp.cos(ang).astype(t.dtype), jnp.sin(ang).astype(t.dtype)
    t1, t2 = t[..., : d // 2], t[..., d // 2:]
    return jnp.concatenate([t1 * cos - t2 * sin, t1 * sin + t2 * cos], axis=-1)


def retention_chunkwise(q, k, v, log_gamma):
    out_dtype = v.dtype
    q, k, v = (a.astype(jnp.float32) for a in (q, k, v))
    b, h, t, dk = q.shape
    dv = v.shape[-1]
    n = t // CHUNK
    qc = q.reshape(b, h, n, CHUNK, dk)
    kc = k.reshape(b, h, n, CHUNK, dk)
    vc = v.reshape(b, h, n, CHUNK, dv)
    idx = jnp.arange(CHUNK, dtype=jnp.float32)
    lg = log_gamma[:, None]
    rel = idx[:, None] - idx[None, :]
    decay = jnp.where(rel >= 0, jnp.exp(lg[..., None] * jnp.maximum(rel, 0.0)), 0.0)
    scores = jnp.einsum('bhnid,bhnjd->bhnij', qc, kc) * decay[None, :, None]
    o_intra = jnp.einsum('bhnij,bhnje->bhnie', scores, vc)
    k_dec = jnp.exp(lg * (CHUNK - 1 - idx))
    u = jnp.einsum('bhnjd,bhnje->bhnde', kc * k_dec[None, :, None, :, None], vc)
    chunk_decay = jnp.exp(log_gamma * CHUNK)[None, :, None, None]

    def step(s, u_n):
        return chunk_decay * s + u_n, s

    _, s_prev = lax.scan(step, jnp.zeros((b, h, dk, dv), jnp.float32), jnp.moveaxis(u, 2, 0))
    s_prev = jnp.moveaxis(s_prev, 0, 2)
    q_dec = jnp.exp(lg * (idx + 1.0))
    o_inter = jnp.einsum('bhnid,bhnde->bhnie', qc * q_dec[None, :, None, :, None], s_prev)
    return (o_intra + o_inter).reshape(b, h, t, dv).astype(out_dtype)


def gated_state_chunkwise(q, k, v, log_f):
    out_dtype = v.dtype
    q, k, v, log_f = (a.astype(jnp.float32) for a in (q, k, v, log_f))
    b, h, t, dk = q.shape
    dv = v.shape[-1]
    n = t // CHUNK

    def to_chunks(a):
        return jnp.moveaxis(a.reshape(b, h, n, CHUNK, a.shape[-1]), 2, 0)

    causal = jnp.tril(jnp.ones((CHUNK, CHUNK), dtype=bool))[:, :, None]

    def step(s, inp):
        qn, kn, vn, gn = inp
        cum = jnp.cumsum(gn, axis=2)
        rel = jnp.where(causal, cum[:, :, :, None, :] - cum[:, :, None, :, :], -jnp.inf)
        attn = jnp.einsum('bhid,bhjd,bhijd->bhij', qn, kn, jnp.exp(rel))
        o = jnp.einsum('bhij,bhje->bhie', attn, vn) + jnp.einsum('bhid,bhde->bhie', qn * jnp.exp(cum), s)
        last = cum[:, :, -1:, :]
        s_new = jnp.exp(last[:, :, 0, :])[..., None] * s + jnp.einsum('bhjd,bhje->bhde', kn * jnp.exp(last - cum), vn)
        return s_new, o

    _, o = lax.scan(step, jnp.zeros((b, h, dk, dv), jnp.float32),
                    (to_chunks(q), to_chunks(k), to_chunks(v), to_chunks(log_f)))
    return jnp.moveaxis(o, 0, 2).reshape(b, h, t, dv).astype(out_dtype)


def token_mixer(h, w_in, w_gla_up, b_gla, lb, head_gain, w_out, pos):
    proj = h @ w_in
    offs = np.cumsum(np.array(IN_SIZES))[:-1].tolist()
    (rq, rk, rv, rg, hq, hf, hi, hg, aq, ak, av, ag, alow) = jnp.split(proj, offs, axis=-1)

    log_gamma = jnp.log1p(-(2.0 ** (-5.0 - jnp.arange(RET_HEADS, dtype=jnp.float32))))
    q_r = rotary(split_heads(rq, RET_HEADS), pos)
    k_r = rotary(split_heads(rk, RET_HEADS), pos) * (RET_DK ** -0.5)
    o_r = retention_chunkwise(q_r, k_r, split_heads(rv, RET_HEADS), log_gamma)
    o_r = merge_heads(rms_norm(o_r)) * jax.nn.silu(rg)

    z = split_heads(hf, HG_HEADS).astype(jnp.float32)
    lbh = lb.reshape(HG_HEADS, HG_DK)[None, :, None, :]
    log_f = jnp.logaddexp(jnp.log1p(-lbh) + jax.nn.log_sigmoid(z), jnp.log(lbh))
    key_h = (1.0 - lbh) * jax.nn.sigmoid(-z)
    o_h = gated_state_chunkwise(jax.nn.silu(split_heads(hq, HG_HEADS)), key_h,
                                split_heads(hi, HG_HEADS), log_f)
    o_h = merge_heads(rms_norm(o_h)) * jax.nn.silu(hg)

    g_log = jax.nn.log_sigmoid((alow @ w_gla_up + b_gla).astype(jnp.float32)) / GLA_GATE_NORM
    o_a = gated_state_chunkwise(split_heads(aq, GLA_HEADS),
                                split_heads(ak, GLA_HEADS) * (GLA_DK ** -0.5),
                                split_heads(av, GLA_HEADS), split_heads(g_log, GLA_HEADS))
    o_a = merge_heads(rms_norm(o_a)) * jax.nn.silu(ag)

    o = jnp.concatenate([o_r, o_h.astype(o_r.dtype), o_a.astype(o_r.dtype)], axis=-1) * head_gain
    return o @ w_out


def conv_ffn(h, w_up, conv_w, conv_b, w_down):
    u = h @ w_up
    u = lax.conv_general_dilated(u, conv_w.reshape(CONV_W, 1, 2 * D_FF), window_strides=(1,),
                                 padding=[(CONV_W - 1, 0)], dimension_numbers=('NWC', 'WIO', 'NWC'),
                                 feature_group_count=2 * D_FF) + conv_b
    val, gate = jnp.split(u, 2, axis=-1)
    return (jax.nn.silu(gate) * val) @ w_down


def setup_inputs(seed: int = 0) -> dict:
    key = jax.random.key(seed)
    ks = jax.random.split(key, 15)
    f32 = jnp.float32

    def nrm(k, shape, s):
        return jax.random.normal(k, shape, f32) * s

    return {
        "x": nrm(ks[0], (BATCH, SEQ, D_MODEL), 1.0),
        "c": nrm(ks[1], (BATCH, D_MODEL), 1.0),
        "w_in": nrm(ks[2], (DEPTH, D_MODEL, N_IN), D_MODEL ** -0.5),
        "w_gla_up": nrm(ks[3], (DEPTH, GLA_RANK, GLA_KW), GLA_RANK ** -0.5),
        "b_gla": nrm(ks[4], (DEPTH, GLA_KW), 0.01),
        "lb_logits": nrm(ks[5], (DEPTH, HG_FD), 0.5),
        "head_gain": 1.0 + nrm(ks[6], (DEPTH, MIX_W), 0.02),
        "w_out": nrm(ks[7], (DEPTH, MIX_W, D_MODEL), MIX_W ** -0.5),
        "w_ada": nrm(ks[8], (DEPTH, D_MODEL, 6 * D_MODEL), 0.5 * D_MODEL ** -0.5),
        "b_ada": nrm(ks[9], (DEPTH, 6 * D_MODEL), 0.01),
        "w_up": nrm(ks[10], (DEPTH, D_MODEL, 2 * D_FF), D_MODEL ** -0.5),
        "conv_w": nrm(ks[11], (DEPTH, CONV_W, 2 * D_FF), CONV_W ** -0.5),
        "conv_b": nrm(ks[12], (DEPTH, 2 * D_FF), 0.01),
        "w_down": nrm(ks[13], (DEPTH, D_FF, D_MODEL), D_FF ** -0.5),
        "final_gain": 1.0 + nrm(ks[14], (D_MODEL,), 0.02),
    }


def reference(x, c, w_in, w_gla_up, b_gla, lb_logits, head_gain, w_out, w_ada, b_ada,
              w_up, conv_w, conv_b, w_down, final_gain):
    pos = jnp.arange(x.shape[1], dtype=jnp.int32)
    lb_all = jnp.cumsum(jax.nn.softmax(lb_logits.astype(jnp.float32), axis=0), axis=0)
    lb_all = lb_all - lb_all[0:1]
    c_act = jax.nn.silu(c)
    for l in range(DEPTH):
        mod = (c_act @ w_ada[l] + b_ada[l])[:, None, :]
        sh1, sc1, g1, sh2, sc2, g2 = jnp.split(mod, 6, axis=-1)
        h = rms_norm(x) * (1.0 + sc1) + sh1
        x = x + g1 * token_mixer(h, w_in[l], w_gla_up[l], b_gla[l], lb_all[l], head_gain[l], w_out[l], pos)
        h = rms_norm(x) * (1.0 + sc2) + sh2
        x = x + g2 * conv_ffn(h, w_up[l], conv_w[l], conv_b[l], w_down[l])
    return rms_norm(x) * final_gain
```

```python
import functools

import numpy as np
import jax
import jax.numpy as jnp
from jax import lax
from jax.experimental import pallas as pl
from jax.experimental.pallas import tpu as pltpu

F32 = jnp.float32
BF16 = jnp.bfloat16

D_MODEL = 1024
DEPTH = 2
CHUNK = 64
EPS = 1e-6
HEAD_DV = 64
RET_HEADS, HG_HEADS, GLA_HEADS = 4, 8, 4
RET_W, HG_W, GLA_W = 256, 512, 256
RET_KW, HG_FD, GLA_KW = 256, 512, 128
RET_DK, GLA_DK = 64, 32
GLA_RANK = 16
GLA_GATE_NORM = 16.0
ROPE_BASE = 10000.0
D_FF = 2816
CONV_W = 3

O_RQ, O_RK, O_RV, O_RG = 0, 256, 512, 768
O_HQ, O_HF, O_HI, O_HG = 1024, 1536, 2048, 2560
O_AQ, O_AK, O_AV, O_AG, O_AL = 3072, 3200, 3328, 3584, 3840
N_IN_PAD = 3968
GW = HG_FD + GLA_KW

LEVELS = (32, 16, 8, 4, 2, 1)
N_MM_LEVELS = 5

TT_MIX = 256
TT_FFN = 256
VMEM_LIMIT = 56 * 1024 * 1024


def _dot(a, b):
    return jnp.dot(a, b, preferred_element_type=F32)


def _dot_nt(a, b):
    return lax.dot_general(a, b, (((1,), (1,)), ((), ())), preferred_element_type=F32)


def _dot_tn(a, b):
    return lax.dot_general(a, b, (((0,), (0,)), ((), ())), preferred_element_type=F32)


def _sigmoid(x):
    return 1.0 / (1.0 + jnp.exp(-x))


def _silu(x):
    return x * _sigmoid(x)


def _log_sigmoid(x):
    return jnp.minimum(x, 0.0) - jnp.log1p(jnp.exp(-jnp.abs(x)))


def _rms(x):
    return x * lax.rsqrt(jnp.mean(x * x, axis=-1, keepdims=True) + EPS)


def _split3(x):
    hi = x.astype(BF16)
    r1 = x - hi.astype(F32)
    mid = r1.astype(BF16)
    lo = (r1 - mid.astype(F32)).astype(BF16)
    return hi, mid, lo


def _split2(x):
    hi = x.astype(BF16)
    lo = (x - hi.astype(F32)).astype(BF16)
    return hi, lo


def _level_constants():
    c = CHUNK
    mats = []
    for s in LEVELS[:N_MM_LEVELS]:
        m = np.zeros((c, c), np.float32)
        for i in range(c):
            ref = (i // (2 * s)) * 2 * s + s - 1
            if (i // s) % 2 == 1:
                m[i, ref + 1:i + 1] = 1.0
            else:
                m[i, i + 1:ref + 1] = 1.0
        mats.append(m)
    mats.append(np.tril(np.ones((c, c), np.float32)))
    mstack = np.concatenate(mats, axis=0)
    m3 = np.tile(mstack, (1, 3))
    masks = []
    ii, jj = np.meshgrid(np.arange(c), np.arange(c), indexing="ij")
    for s in LEVELS:
        mk = ((ii // (2 * s)) == (jj // (2 * s))) & ((ii // s) % 2 == 1) & ((jj // s) % 2 == 0)
        masks.append(np.tile(mk.astype(np.float32), (1, 4)))
    return m3, np.stack(masks)


def _block_masks():
    r = np.arange(256)[:, None] // 64
    bd64 = (r == (np.arange(256)[None, :] // 64)).astype(np.float32)
    bdg = (r == (np.arange(128)[None, :] // 32)).astype(np.float32)
    bdr = (r == ((np.arange(256)[None, :] % 128) // 32)).astype(np.float32)
    return bd64, bdg, bdr


def _retention_tables():
    h = jnp.arange(RET_HEADS, dtype=F32)
    lg = jnp.log1p(-(2.0 ** (-5.0 - h)))
    idx = jnp.arange(CHUNK, dtype=F32)
    rel = idx[:, None] - idx[None, :]
    decay = jnp.where(rel >= 0, jnp.exp(lg[:, None, None] * jnp.maximum(rel, 0.0)), 0.0)
    dmat = jnp.transpose(decay, (1, 0, 2)).reshape(CHUNK, RET_HEADS * CHUNK)
    lane_head = (jnp.arange(256) % 128) // 32
    lg_lane = lg[lane_head]
    qdec = jnp.exp(lg_lane[None, :] * (idx[:, None] + 1.0))
    kdec = jnp.exp(lg_lane[None, :] * (CHUNK - 1 - idx[:, None]))
    cdec = jnp.exp(lg_lane * CHUNK)[None, :]
    return dmat, qdec, kdec, cdec


def _rotary_tables(seq):
    half = RET_DK // 2
    inv = ROPE_BASE ** (-jnp.arange(0, RET_DK, 2, dtype=F32) / RET_DK)
    ang = jnp.arange(seq, dtype=jnp.int32).astype(F32)[:, None] * inv[None, :]
    cos = jnp.tile(jnp.cos(ang), (1, 128 // half))
    sin = jnp.tile(jnp.sin(ang), (1, 128 // half))
    return cos, sin


def _pack_w_in(w):
    def rot_perm(block):
        b = block.reshape(D_MODEL, RET_HEADS, 2, RET_DK // 2)
        return jnp.transpose(b, (0, 2, 1, 3)).reshape(D_MODEL, RET_KW)
    rq = rot_perm(w[:, 0:256])
    rk = rot_perm(w[:, 256:512])
    rest = w[:, 512:3856]
    pad = jnp.zeros((D_MODEL, N_IN_PAD - 3856), w.dtype)
    return jnp.concatenate([rq, rk, rest, pad], axis=1).astype(BF16)


def _prep_kernel(c_ref, w_ref, b_ref, lbl_ref, mod_ref, lb_ref):
    ca = _silu(c_ref[...])
    w = w_ref[0]
    c_hi, c_lo = _split2(ca)
    w_hi, w_lo = _split2(w)
    acc = _dot(c_hi, w_hi) + _dot(c_lo, w_hi) + _dot(c_hi, w_lo)
    mod_ref[0] = acc + b_ref[0]
    rows = [lbl_ref[l] for l in range(DEPTH)]
    mx = functools.reduce(jnp.maximum, rows)
    ex = [jnp.exp(r - mx) for r in rows]
    den = functools.reduce(lambda a, b: a + b, ex)
    run = None
    first = None
    for l in range(DEPTH):
        sm = ex[l] / den
        run = sm if run is None else run + sm
        if first is None:
            first = run
        lb_ref[l] = run - first


def _prep(c, w_ada, b_ada, lb_logits):
    b = c.shape[0]
    nblk = 6 * D_MODEL // 1024
    return pl.pallas_call(
        _prep_kernel,
        grid=(DEPTH, nblk),
        in_specs=[
            pl.BlockSpec((b, D_MODEL), lambda l, j: (0, 0)),
            pl.BlockSpec((1, D_MODEL, 1024), lambda l, j: (l, 0, j)),
            pl.BlockSpec((1, 1, 1024), lambda l, j: (l, 0, j)),
            pl.BlockSpec((DEPTH, 1, HG_FD), lambda l, j: (0, 0, 0)),
        ],
        out_specs=[
            pl.BlockSpec((1, b, 1024), lambda l, j: (l, 0, j)),
            pl.BlockSpec((DEPTH, 1, HG_FD), lambda l, j: (0, 0, 0)),
        ],
        out_shape=[
            jax.ShapeDtypeStruct((DEPTH, b, 6 * D_MODEL), F32),
            jax.ShapeDtypeStruct((DEPTH, 1, HG_FD), F32),
        ],
        compiler_params=pltpu.CompilerParams(dimension_semantics=("arbitrary", "arbitrary")),
        name="prep",
    )(c, w_ada, b_ada.reshape(DEPTH, 1, 6 * D_MODEL), lb_logits.reshape(DEPTH, 1, HG_FD))


def _mixer_kernel(x_ref, mod_ref, w_in_ref, w_out_ref, wg_ref, bg_ref, lb_ref, gain_ref,
                  cos_ref, sin_ref, m3_ref, lvl_ref, bd64b_ref, bdgb_ref, bdrb_ref,
                  bd64f_ref, bdgf_ref, bdrf_ref, dmat_ref, qdec_ref, kdec_ref, cdec_ref,
                  out_ref,
                  proj_ref, lf_ref, qg_ref, kg_ref, qr_ref, kr_ref, o_ref,
                  st_r_ref, st_h0_ref, st_h1_ref, st_g_ref):
    tt = x_ref.shape[1]

    @pl.when(pl.program_id(1) == 0)
    def _():
        st_r_ref[...] = jnp.zeros_like(st_r_ref)
        st_h0_ref[...] = jnp.zeros_like(st_h0_ref)
        st_h1_ref[...] = jnp.zeros_like(st_h1_ref)
        st_g_ref[...] = jnp.zeros_like(st_g_ref)

    mod = mod_ref[0]
    sh1 = mod[:, 0:D_MODEL]
    sc1 = mod[:, D_MODEL:2 * D_MODEL]
    g1 = mod[:, 2 * D_MODEL:3 * D_MODEL]

    h = (_rms(x_ref[0]) * (1.0 + sc1) + sh1).astype(BF16)
    for c0 in range(0, N_IN_PAD, 512):
        c1 = min(c0 + 512, N_IN_PAD)
        proj_ref[:, c0:c1] = _dot(h, w_in_ref[:, c0:c1])

    cos = cos_ref[...]
    sin = sin_ref[...]
    q1 = proj_ref[:, O_RQ:O_RQ + 128]
    q2 = proj_ref[:, O_RQ + 128:O_RQ + 256]
    k1 = proj_ref[:, O_RK:O_RK + 128]
    k2 = proj_ref[:, O_RK + 128:O_RK + 256]
    kscale = RET_DK ** -0.5
    qr_ref[:, 0:128] = q1 * cos - q2 * sin
    qr_ref[:, 128:256] = q1 * sin + q2 * cos
    kr_ref[:, 0:128] = (k1 * cos - k2 * sin) * kscale
    kr_ref[:, 128:256] = (k1 * sin + k2 * cos) * kscale

    lb = lb_ref[0]
    z = proj_ref[:, O_HF:O_HF + HG_FD]
    a = jnp.log1p(-lb) + _log_sigmoid(z)
    lb_pos = lb > 0.0
    b = jnp.log(jnp.where(lb_pos, lb, 1.0))
    both = jnp.maximum(a, b) + jnp.log1p(jnp.exp(-jnp.abs(a - b)))
    lf_ref[:, 0:HG_FD] = jnp.where(lb_pos, both, a)
    kg_ref[:, 0:HG_FD] = (1.0 - lb) * _sigmoid(-z)
    qg_ref[:, 0:HG_FD] = _silu(proj_ref[:, O_HQ:O_HQ + HG_FD])

    alow = proj_ref[:, O_AL:O_AL + 128].astype(BF16)
    glogit = _dot(alow, wg_ref[...]) + bg_ref[...]
    lf_ref[:, HG_FD:GW] = _log_sigmoid(glogit) * (1.0 / GLA_GATE_NORM)
    qg_ref[:, HG_FD:GW] = proj_ref[:, O_AQ:O_AQ + GLA_KW]
    kg_ref[:, HG_FD:GW] = proj_ref[:, O_AK:O_AK + GLA_KW] * (GLA_DK ** -0.5)

    row = lax.broadcasted_iota(jnp.int32, (CHUNK, GW), 0)
    bd64b = bd64b_ref[...]
    bdgb = bdgb_ref[...]
    bdrb = bdrb_ref[...]

    groups = (
        (0, 256, O_HI, 256, st_h0_ref, bd64b, bd64f_ref),
        (256, 512, O_HI + 256, 512, st_h1_ref, bd64b, bd64f_ref),
        (512, 640, O_AV, 768, st_g_ref, bdgb, bdgf_ref),
    )

    def chunk_body(ci, carry):
        r0 = pl.multiple_of(ci * CHUNK, CHUNK)
        rows = pl.ds(r0, CHUNK)

        qr = qr_ref[rows, :]
        kr = kr_ref[rows, :]
        vr = proj_ref[rows, O_RV:O_RV + RET_W].astype(BF16)
        qrb = qr.astype(BF16)
        krb = kr.astype(BF16)
        kst = jnp.concatenate([krb] * 4, axis=0) * bdrb
        p = _dot_nt(qrb, kst) * dmat_ref[...]
        vst = jnp.concatenate([vr] * 4, axis=0) * bd64b
        o_r = _dot(p.astype(BF16), vst)
        st = st_r_ref[...]
        o_r = o_r + _dot_nt((qr * qdec_ref[...]).astype(BF16), st.astype(BF16))
        kv = _dot_tn(vr, (kr * kdec_ref[...]).astype(BF16))
        st_r_ref[...] = st * cdec_ref[...] + kv * bdrf_ref[...]
        o_ref[rows, 0:RET_W] = o_r

        lf = lf_ref[rows, :]
        qg = qg_ref[rows, :]
        kg = kg_ref[rows, :]
        hi, mid, lo = _split3(lf)
        gmat = _dot(m3_ref[...], jnp.concatenate([hi, mid, lo], axis=0))
        acc = [jnp.zeros((CHUNK, 256), F32) for _ in groups]
        for li, s in enumerate(LEVELS):
            if li < N_MM_LEVELS:
                gs = gmat[li * CHUNK:(li + 1) * CHUNK, :]
            else:
                gs = jnp.where((row & 1) != 0, lf, 0.0)
            zs = (jnp.where((row & s) != 0, qg, kg) * jnp.exp(gs)).astype(BF16)
            lm = lvl_ref[li]
            for gi, (l0, l1, _, _, _, bdk, _) in enumerate(groups):
                zg = zs[:, l0:l1]
                kst = jnp.concatenate([zg] * 4, axis=0) * bdk
                acc[gi] = acc[gi] + _dot_nt(zg, kst) * lm
        cum = gmat[N_MM_LEVELS * CHUNK:(N_MM_LEVELS + 1) * CHUNK, :]
        last = cum[CHUNK - 1:CHUNK, :]
        qc = (qg * jnp.exp(cum)).astype(BF16)
        kc = (kg * jnp.exp(last - cum)).astype(BF16)
        elast = jnp.exp(last)
        qk = (qg * kg).astype(BF16)
        for gi, (l0, l1, vcol, ocol, st_ref, bdk, bdf_ref) in enumerate(groups):
            v = proj_ref[rows, vcol:vcol + 256]
            vb = v.astype(BF16)
            vst = jnp.concatenate([vb] * 4, axis=0) * bd64b
            o_g = _dot(acc[gi].astype(BF16), vst)
            o_g = o_g + _dot_nt(qk[:, l0:l1], bdk) * v
            st = st_ref[...]
            o_g = o_g + _dot_nt(qc[:, l0:l1], st.astype(BF16))
            kv = _dot_tn(vb, kc[:, l0:l1])
            st_ref[...] = st * elast[:, l0:l1] + kv * bdf_ref[...]
            o_ref[rows, ocol:ocol + 256] = o_g
        return carry

    lax.fori_loop(0, tt // CHUNK, chunk_body, 0)

    gain = gain_ref[...]
    gate_cols = (O_RG, O_HG, O_HG + 256, O_AG)
    ys = []
    for bi in range(4):
        o = o_ref[:, bi * 256:(bi + 1) * 256]
        s_hi, s_lo = _split2(o * o)
        ms = (_dot(s_hi, bd64b) + _dot(s_lo, bd64b)) * (1.0 / HEAD_DV)
        gate = proj_ref[:, gate_cols[bi]:gate_cols[bi] + 256]
        y = o * lax.rsqrt(ms + EPS) * _silu(gate) * gain[:, bi * 256:(bi + 1) * 256]
        ys.append(y.astype(BF16))
    y = jnp.concatenate(ys, axis=1)
    out_ref[0] = x_ref[0] + g1 * _dot(y, w_out_ref[...])


def _const_spec(shape):
    nd = len(shape)
    return pl.BlockSpec(shape, lambda b, t, _nd=nd: (0,) * _nd, pipeline_mode=pl.Buffered(1))


def _mixer(x, mod_l, w_in_p, w_out_b, wg, bg, lb_l, gain, consts):
    bsz, seq, _ = x.shape
    tt = TT_MIX
    (cos, sin, m3, lvl, bd64b, bdgb, bdrb, bd64f, bdgf, bdrf, dmat, qdec, kdec, cdec) = consts
    const_inputs = (m3, lvl, bd64b, bdgb, bdrb, bd64f, bdgf, bdrf, dmat, qdec, kdec, cdec)
    in_specs = [
        pl.BlockSpec((1, tt, D_MODEL), lambda b, t: (b, t, 0)),
        pl.BlockSpec((1, 1, 6 * D_MODEL), lambda b, t: (b, 0, 0)),
        _const_spec(w_in_p.shape),
        _const_spec(w_out_b.shape),
        _const_spec(wg.shape),
        _const_spec(bg.shape),
        _const_spec(lb_l.shape),
        _const_spec(gain.shape),
        pl.BlockSpec((tt, 128), lambda b, t: (t, 0)),
        pl.BlockSpec((tt, 128), lambda b, t: (t, 0)),
    ] + [_const_spec(a.shape) for a in const_inputs]
    scratch = [
        pltpu.VMEM((tt, N_IN_PAD), F32),
        pltpu.VMEM((tt, GW), F32),
        pltpu.VMEM((tt, GW), F32),
        pltpu.VMEM((tt, GW), F32),
        pltpu.VMEM((tt, RET_KW), F32),
        pltpu.VMEM((tt, RET_KW), F32),
        pltpu.VMEM((tt, D_MODEL), F32),
        pltpu.VMEM((256, 256), F32),
        pltpu.VMEM((256, 256), F32),
        pltpu.VMEM((256, 256), F32),
        pltpu.VMEM((256, 128), F32),
    ]
    return pl.pallas_call(
        _mixer_kernel,
        grid=(bsz, seq // tt),
        in_specs=in_specs,
        out_specs=pl.BlockSpec((1, tt, D_MODEL), lambda b, t: (b, t, 0)),
        out_shape=jax.ShapeDtypeStruct(x.shape, F32),
        scratch_shapes=scratch,
        compiler_params=pltpu.CompilerParams(
            dimension_semantics=("arbitrary", "arbitrary"), vmem_limit_bytes=VMEM_LIMIT),
        name="mixer",
    )(x, mod_l, w_in_p, w_out_b, wg, bg, lb_l, gain, cos, sin, *const_inputs)


def _ffn_kernel(x_ref, mod_ref, w_up_ref, cw_ref, cb_ref, w_down_ref, fg_ref, out_ref, u_ref, *, final):
    tt = x_ref.shape[1]
    first = pl.program_id(1) == 0

    mod = mod_ref[0]
    sh2 = mod[:, 3 * D_MODEL:4 * D_MODEL]
    sc2 = mod[:, 4 * D_MODEL:5 * D_MODEL]
    g2 = mod[:, 5 * D_MODEL:6 * D_MODEL]

    @pl.when(first)
    def _():
        u_ref[0:8, :] = jnp.zeros((8, 2 * D_FF), F32)

    @pl.when(jnp.logical_not(first))
    def _():
        u_ref[0:8, :] = u_ref[tt:tt + 8, :]

    x = x_ref[0]
    h = (_rms(x) * (1.0 + sc2) + sh2).astype(BF16)
    cblk = 512
    for c0 in range(0, 2 * D_FF, cblk):
        u_ref[8:8 + tt, c0:c0 + cblk] = _dot(h, w_up_ref[:, c0:c0 + cblk])

    acc = jnp.zeros((tt, D_MODEL), F32)
    fblk = 256
    for c0 in range(0, D_FF, fblk):
        def conv(col):
            w = cw_ref[:, col:col + fblk]
            return (u_ref[6:6 + tt, col:col + fblk] * w[0:1, :]
                    + u_ref[7:7 + tt, col:col + fblk] * w[1:2, :]
                    + u_ref[8:8 + tt, col:col + fblk] * w[2:3, :]
                    + cb_ref[:, col:col + fblk])
        val = conv(c0)
        gate = conv(D_FF + c0)
        act = (_silu(gate) * val).astype(BF16)
        acc = acc + _dot(act, w_down_ref[c0:c0 + fblk, :])
    y = x + g2 * acc
    if final:
        y = _rms(y) * fg_ref[...]
    out_ref[0] = y


def _ffn(x, mod_l, w_up_b, conv_w, conv_b, w_down_b, final_gain, final):
    bsz, seq, _ = x.shape
    tt = TT_FFN
    return pl.pallas_call(
        functools.partial(_ffn_kernel, final=final),
        grid=(bsz, seq // tt),
        in_specs=[
            pl.BlockSpec((1, tt, D_MODEL), lambda b, t: (b, t, 0)),
            pl.BlockSpec((1, 1, 6 * D_MODEL), lambda b, t: (b, 0, 0)),
            _const_spec(w_up_b.shape),
            _const_spec(conv_w.shape),
            _const_spec(conv_b.shape),
            _const_spec(w_down_b.shape),
            _const_spec(final_gain.shape),
        ],
        out_specs=pl.BlockSpec((1, tt, D_MODEL), lambda b, t: (b, t, 0)),
        out_shape=jax.ShapeDtypeStruct(x.shape, F32),
        scratch_shapes=[pltpu.VMEM((tt + 8, 2 * D_FF), F32)],
        compiler_params=pltpu.CompilerParams(
            dimension_semantics=("arbitrary", "arbitrary"), vmem_limit_bytes=VMEM_LIMIT),
        name="ffn",
    )(x, mod_l, w_up_b, conv_w, conv_b, w_down_b, final_gain)


def kernel(x, c, w_in, w_gla_up, b_gla, lb_logits, head_gain, w_out, w_ada, b_ada,
           w_up, conv_w, conv_b, w_down, final_gain):
    bsz, seq, _ = x.shape
    m3, lvl = _level_constants()
    bd64, bdg, bdr = _block_masks()
    cos, sin = _rotary_tables(seq)
    dmat, qdec, kdec, cdec = _retention_tables()
    consts = (cos, sin, jnp.asarray(m3, BF16), jnp.asarray(lvl, F32),
              jnp.asarray(bd64, BF16), jnp.asarray(bdg, BF16), jnp.asarray(bdr, BF16),
              jnp.asarray(bd64, F32), jnp.asarray(bdg, F32), jnp.asarray(bdr, F32),
              dmat, qdec, kdec, cdec)

    mod, lb_all = _prep(c, w_ada, b_ada, lb_logits)
    wg_pad = jnp.zeros((DEPTH, 128, GLA_KW), F32).at[:, :GLA_RANK, :].set(w_gla_up).astype(BF16)
    fg = final_gain.reshape(1, D_MODEL)

    for l in range(DEPTH):
        mod_l = mod[l].reshape(bsz, 1, 6 * D_MODEL)
        x = _mixer(x, mod_l, _pack_w_in(w_in[l]), w_out[l].astype(BF16), wg_pad[l],
                   b_gla[l].reshape(1, GLA_KW), lb_all[l], head_gain[l].reshape(1, D_MODEL), consts)
        x = _ffn(x, mod_l, w_up[l].astype(BF16), conv_w[l], conv_b[l].reshape(1, 2 * D_FF),
                 w_down[l].astype(BF16), fg, final=(l == DEPTH - 1))
    return x
```

```python
import functools

import numpy as np
import jax
import jax.numpy as jnp
from jax import lax
from jax.experimental import pallas as pl
from jax.experimental.pallas import tpu as pltpu

F32 = jnp.float32
BF16 = jnp.bfloat16

D_MODEL = 1024
DEPTH = 2
CHUNK = 64
EPS = 1e-6
HEAD_DV = 64
RET_HEADS, HG_HEADS, GLA_HEADS = 4, 8, 4
RET_W, HG_W, GLA_W = 256, 512, 256
RET_KW, HG_FD, GLA_KW = 256, 512, 128
RET_DK, GLA_DK = 64, 32
GLA_RANK = 16
GLA_GATE_NORM = 16.0
ROPE_BASE = 10000.0
D_FF = 2816
CONV_W = 3

O_RQ, O_RK, O_RV, O_RG = 0, 256, 512, 768
O_HQ, O_HF, O_HI, O_HG = 1024, 1536, 2048, 2560
O_AQ, O_AK, O_AV, O_AG, O_AL = 3072, 3200, 3328, 3584, 3840
N_IN_PAD = 3968
GW = HG_FD + GLA_KW

LEVELS = (32, 16, 8, 4, 2, 1)
N_MM_LEVELS = 5

TT_MIX = 256
TT_FFN = 512
VMEM_LIMIT = 56 * 1024 * 1024


def _dot(a, b):
    return jnp.dot(a, b, preferred_element_type=F32)


def _dot_nt(a, b):
    return lax.dot_general(a, b, (((1,), (1,)), ((), ())), preferred_element_type=F32)


def _dot_tn(a, b):
    return lax.dot_general(a, b, (((0,), (0,)), ((), ())), preferred_element_type=F32)


def _sigmoid(x):
    return 1.0 / (1.0 + jnp.exp(-x))


def _silu(x):
    return x * _sigmoid(x)


def _log_sigmoid(x):
    return jnp.minimum(x, 0.0) - jnp.log1p(jnp.exp(-jnp.abs(x)))


def _rms(x):
    return x * lax.rsqrt(jnp.mean(x * x, axis=-1, keepdims=True) + EPS)


def _split3(x):
    hi = x.astype(BF16)
    r1 = x - hi.astype(F32)
    mid = r1.astype(BF16)
    lo = (r1 - mid.astype(F32)).astype(BF16)
    return hi, mid, lo


def _split2(x):
    hi = x.astype(BF16)
    lo = (x - hi.astype(F32)).astype(BF16)
    return hi, lo


def _level_constants():
    c = CHUNK
    mats = []
    for s in LEVELS[:N_MM_LEVELS]:
        m = np.zeros((c, c), np.float32)
        for i in range(c):
            ref = (i // (2 * s)) * 2 * s + s - 1
            if (i // s) % 2 == 1:
                m[i, ref + 1:i + 1] = 1.0
            else:
                m[i, i + 1:ref + 1] = 1.0
        mats.append(m)
    mats.append(np.tril(np.ones((c, c), np.float32)))
    mstack = np.concatenate(mats, axis=0)
    m3 = np.tile(mstack, (1, 3))
    masks = []
    ii, jj = np.meshgrid(np.arange(c), np.arange(c), indexing="ij")
    for s in LEVELS:
        mk = ((ii // (2 * s)) == (jj // (2 * s))) & ((ii // s) % 2 == 1) & ((jj // s) % 2 == 0)
        masks.append(np.tile(mk.astype(np.float32), (1, 4)))
    return m3, np.stack(masks)


def _block_masks():
    r = np.arange(256)[:, None] // 64
    bd64 = (r == (np.arange(256)[None, :] // 64)).astype(np.float32)
    bdg = (r == (np.arange(128)[None, :] // 32)).astype(np.float32)
    bdr = (r == ((np.arange(256)[None, :] % 128) // 32)).astype(np.float32)
    return bd64, bdg, bdr


def _retention_tables():
    h = jnp.arange(RET_HEADS, dtype=F32)
    lg = jnp.log1p(-(2.0 ** (-5.0 - h)))
    idx = jnp.arange(CHUNK, dtype=F32)
    rel = idx[:, None] - idx[None, :]
    decay = jnp.where(rel >= 0, jnp.exp(lg[:, None, None] * jnp.maximum(rel, 0.0)), 0.0)
    dmat = jnp.transpose(decay, (1, 0, 2)).reshape(CHUNK, RET_HEADS * CHUNK)
    lane_head = (jnp.arange(256) % 128) // 32
    lg_lane = lg[lane_head]
    qdec = jnp.exp(lg_lane[None, :] * (idx[:, None] + 1.0))
    kdec = jnp.exp(lg_lane[None, :] * (CHUNK - 1 - idx[:, None]))
    cdec = jnp.exp(lg_lane * CHUNK)[None, :]
    return dmat, qdec, kdec, cdec


def _rotary_tables(seq):
    half = RET_DK // 2
    inv = ROPE_BASE ** (-jnp.arange(0, RET_DK, 2, dtype=F32) / RET_DK)
    ang = jnp.arange(seq, dtype=jnp.int32).astype(F32)[:, None] * inv[None, :]
    cos = jnp.tile(jnp.cos(ang), (1, 128 // half))
    sin = jnp.tile(jnp.sin(ang), (1, 128 // half))
    return cos, sin


def _pack_w_in(w):
    def rot_perm(block):
        b = block.reshape(D_MODEL, RET_HEADS, 2, RET_DK // 2)
        return jnp.transpose(b, (0, 2, 1, 3)).reshape(D_MODEL, RET_KW)
    rq = rot_perm(w[:, 0:256])
    rk = rot_perm(w[:, 256:512])
    rest = w[:, 512:3856]
    pad = jnp.zeros((D_MODEL, N_IN_PAD - 3856), w.dtype)
    return jnp.concatenate([rq, rk, rest, pad], axis=1).astype(BF16)


def _prep_kernel(c_ref, w_ref, b_ref, lbl_ref, mod_ref, lb_ref):
    ca = _silu(c_ref[...])
    w = w_ref[0]
    c_hi, c_lo = _split2(ca)
    w_hi, w_lo = _split2(w)
    acc = _dot(c_hi, w_hi) + _dot(c_lo, w_hi) + _dot(c_hi, w_lo)
    mod_ref[0] = acc + b_ref[0]
    rows = [lbl_ref[l] for l in range(DEPTH)]
    mx = functools.reduce(jnp.maximum, rows)
    ex = [jnp.exp(r - mx) for r in rows]
    den = functools.reduce(lambda a, b: a + b, ex)
    run = None
    first = None
    for l in range(DEPTH):
        sm = ex[l] / den
        run = sm if run is None else run + sm
        if first is None:
            first = run
        lb_ref[l] = run - first


def _prep(c, w_ada, b_ada, lb_logits):
    b = c.shape[0]
    nblk = 6 * D_MODEL // 1024
    return pl.pallas_call(
        _prep_kernel,
        grid=(DEPTH, nblk),
        in_specs=[
            pl.BlockSpec((b, D_MODEL), lambda l, j: (0, 0)),
            pl.BlockSpec((1, D_MODEL, 1024), lambda l, j: (l, 0, j)),
            pl.BlockSpec((1, 1, 1024), lambda l, j: (l, 0, j)),
            pl.BlockSpec((DEPTH, 1, HG_FD), lambda l, j: (0, 0, 0)),
        ],
        out_specs=[
            pl.BlockSpec((1, b, 1024), lambda l, j: (l, 0, j)),
            pl.BlockSpec((DEPTH, 1, HG_FD), lambda l, j: (0, 0, 0)),
        ],
        out_shape=[
            jax.ShapeDtypeStruct((DEPTH, b, 6 * D_MODEL), F32),
            jax.ShapeDtypeStruct((DEPTH, 1, HG_FD), F32),
        ],
        compiler_params=pltpu.CompilerParams(dimension_semantics=("arbitrary", "arbitrary")),
        name="prep",
    )(c, w_ada, b_ada.reshape(DEPTH, 1, 6 * D_MODEL), lb_logits.reshape(DEPTH, 1, HG_FD))


def _mixer_kernel(x_ref, mod_ref, w_in_ref, w_out_ref, wg_ref, bg_ref, lb_ref, gain_ref,
                  cos_ref, sin_ref, m3_ref, lvl_ref, bd64b_ref, bdgb_ref, bdrb_ref,
                  bd64f_ref, bdgf_ref, bdrf_ref, dmat_ref, qdec_ref, kdec_ref, cdec_ref,
                  out_ref,
                  proj_ref, lf_ref, qg_ref, kg_ref, qr_ref, kr_ref, o_ref,
                  st_r_ref, st_h0_ref, st_h1_ref, st_g_ref):
    tt = x_ref.shape[1]

    @pl.when(pl.program_id(1) == 0)
    def _():
        st_r_ref[...] = jnp.zeros_like(st_r_ref)
        st_h0_ref[...] = jnp.zeros_like(st_h0_ref)
        st_h1_ref[...] = jnp.zeros_like(st_h1_ref)
        st_g_ref[...] = jnp.zeros_like(st_g_ref)

    mod = mod_ref[0]
    sh1 = mod[:, 0:D_MODEL]
    sc1 = mod[:, D_MODEL:2 * D_MODEL]
    g1 = mod[:, 2 * D_MODEL:3 * D_MODEL]

    h = (_rms(x_ref[0]) * (1.0 + sc1) + sh1).astype(BF16)
    for c0 in range(0, N_IN_PAD, 512):
        c1 = min(c0 + 512, N_IN_PAD)
        proj_ref[:, c0:c1] = _dot(h, w_in_ref[:, c0:c1])

    cos = cos_ref[...]
    sin = sin_ref[...]
    q1 = proj_ref[:, O_RQ:O_RQ + 128]
    q2 = proj_ref[:, O_RQ + 128:O_RQ + 256]
    k1 = proj_ref[:, O_RK:O_RK + 128]
    k2 = proj_ref[:, O_RK + 128:O_RK + 256]
    kscale = RET_DK ** -0.5
    qr_ref[:, 0:128] = q1 * cos - q2 * sin
    qr_ref[:, 128:256] = q1 * sin + q2 * cos
    kr_ref[:, 0:128] = (k1 * cos - k2 * sin) * kscale
    kr_ref[:, 128:256] = (k1 * sin + k2 * cos) * kscale

    lb = lb_ref[0]
    z = proj_ref[:, O_HF:O_HF + HG_FD]
    a = jnp.log1p(-lb) + _log_sigmoid(z)
    lb_pos = lb > 0.0
    b = jnp.log(jnp.where(lb_pos, lb, 1.0))
    both = jnp.maximum(a, b) + jnp.log1p(jnp.exp(-jnp.abs(a - b)))
    lf_ref[:, 0:HG_FD] = jnp.where(lb_pos, both, a)
    kg_ref[:, 0:HG_FD] = (1.0 - lb) * _sigmoid(-z)
    qg_ref[:, 0:HG_FD] = _silu(proj_ref[:, O_HQ:O_HQ + HG_FD])

    alow = proj_ref[:, O_AL:O_AL + 128].astype(BF16)
    glogit = _dot(alow, wg_ref[...]) + bg_ref[...]
    lf_ref[:, HG_FD:GW] = _log_sigmoid(glogit) * (1.0 / GLA_GATE_NORM)
    qg_ref[:, HG_FD:GW] = proj_ref[:, O_AQ:O_AQ + GLA_KW]
    kg_ref[:, HG_FD:GW] = proj_ref[:, O_AK:O_AK + GLA_KW] * (GLA_DK ** -0.5)

    row = lax.broadcasted_iota(jnp.int32, (CHUNK, GW), 0)
    bd64b = bd64b_ref[...]
    bdgb = bdgb_ref[...]
    bdrb = bdrb_ref[...]

    groups = (
        (0, 256, O_HI, 256, st_h0_ref, bd64b, bd64f_ref),
        (256, 512, O_HI + 256, 512, st_h1_ref, bd64b, bd64f_ref),
        (512, 640, O_AV, 768, st_g_ref, bdgb, bdgf_ref),
    )

    def chunk_body(ci, carry):
        r0 = pl.multiple_of(ci * CHUNK, CHUNK)
        rows = pl.ds(r0, CHUNK)

        qr = qr_ref[rows, :]
        kr = kr_ref[rows, :]
        vr = proj_ref[rows, O_RV:O_RV + RET_W].astype(BF16)
        qrb = qr.astype(BF16)
        krb = kr.astype(BF16)
        kst = jnp.concatenate([krb] * 4, axis=0) * bdrb
        p = _dot_nt(qrb, kst) * dmat_ref[...]
        vst = jnp.concatenate([vr] * 4, axis=0) * bd64b
        o_r = _dot(p.astype(BF16), vst)
        st = st_r_ref[...]
        o_r = o_r + _dot_nt((qr * qdec_ref[...]).astype(BF16), st.astype(BF16))
        kv = _dot_tn(vr, (kr * kdec_ref[...]).astype(BF16))
        st_r_ref[...] = st * cdec_ref[...] + kv * bdrf_ref[...]
        o_ref[rows, 0:RET_W] = o_r

        lf = lf_ref[rows, :]
        qg = qg_ref[rows, :]
        kg = kg_ref[rows, :]
        hi, mid, lo = _split3(lf)
        gmat = _dot(m3_ref[...], jnp.concatenate([hi, mid, lo], axis=0))
        acc = [jnp.zeros((CHUNK, 256), F32) for _ in groups]
        for li, s in enumerate(LEVELS):
            if li < N_MM_LEVELS:
                gs = gmat[li * CHUNK:(li + 1) * CHUNK, :]
            else:
                gs = jnp.where((row & 1) != 0, lf, 0.0)
            zs = (jnp.where((row & s) != 0, qg, kg) * jnp.exp(gs)).astype(BF16)
            lm = lvl_ref[li]
            for gi, (l0, l1, _, _, _, bdk, _) in enumerate(groups):
                zg = zs[:, l0:l1]
                kst = jnp.concatenate([zg] * 4, axis=0) * bdk
                acc[gi] = acc[gi] + _dot_nt(zg, kst) * lm
        cum = gmat[N_MM_LEVELS * CHUNK:(N_MM_LEVELS + 1) * CHUNK, :]
        last = cum[CHUNK - 1:CHUNK, :]
        qc = (qg * jnp.exp(cum)).astype(BF16)
        kc = (kg * jnp.exp(last - cum)).astype(BF16)
        elast = jnp.exp(last)
        qk = (qg * kg).astype(BF16)
        for gi, (l0, l1, vcol, ocol, st_ref, bdk, bdf_ref) in enumerate(groups):
            v = proj_ref[rows, vcol:vcol + 256]
            vb = v.astype(BF16)
            vst = jnp.concatenate([vb] * 4, axis=0) * bd64b
            o_g = _dot(acc[gi].astype(BF16), vst)
            o_g = o_g + _dot_nt(qk[:, l0:l1], bdk) * v
            st = st_ref[...]
            o_g = o_g + _dot_nt(qc[:, l0:l1], st.astype(BF16))
            kv = _dot_tn(vb, kc[:, l0:l1])
            st_ref[...] = st * elast[:, l0:l1] + kv * bdf_ref[...]
            o_ref[rows, ocol:ocol + 256] = o_g
        return carry

    lax.fori_loop(0, tt // CHUNK, chunk_body, 0, unroll=True)

    gain = gain_ref[...]
    gate_cols = (O_RG, O_HG, O_HG + 256, O_AG)
    ys = []
    for bi in range(4):
        o = o_ref[:, bi * 256:(bi + 1) * 256]
        s_hi, s_lo = _split2(o * o)
        ms = (_dot(s_hi, bd64b) + _dot(s_lo, bd64b)) * (1.0 / HEAD_DV)
        gate = proj_ref[:, gate_cols[bi]:gate_cols[bi] + 256]
        y = o * lax.rsqrt(ms + EPS) * _silu(gate) * gain[:, bi * 256:(bi + 1) * 256]
        ys.append(y.astype(BF16))
    y = jnp.concatenate(ys, axis=1)
    out_ref[0] = x_ref[0] + g1 * _dot(y, w_out_ref[...])


def _const_spec(shape):
    nd = len(shape)
    return pl.BlockSpec(shape, lambda b, t, _nd=nd: (0,) * _nd, pipeline_mode=pl.Buffered(1))


def _mixer(x, mod_l, w_in_p, w_out_b, wg, bg, lb_l, gain, consts):
    bsz, seq, _ = x.shape
    tt = TT_MIX
    (cos, sin, m3, lvl, bd64b, bdgb, bdrb, bd64f, bdgf, bdrf, dmat, qdec, kdec, cdec) = consts
    const_inputs = (m3, lvl, bd64b, bdgb, bdrb, bd64f, bdgf, bdrf, dmat, qdec, kdec, cdec)
    in_specs = [
        pl.BlockSpec((1, tt, D_MODEL), lambda b, t: (b, t, 0)),
        pl.BlockSpec((1, 1, 6 * D_MODEL), lambda b, t: (b, 0, 0)),
        _const_spec(w_in_p.shape),
        _const_spec(w_out_b.shape),
        _const_spec(wg.shape),
        _const_spec(bg.shape),
        _const_spec(lb_l.shape),
        _const_spec(gain.shape),
        pl.BlockSpec((tt, 128), lambda b, t: (t, 0)),
        pl.BlockSpec((tt, 128), lambda b, t: (t, 0)),
    ] + [_const_spec(a.shape) for a in const_inputs]
    scratch = [
        pltpu.VMEM((tt, N_IN_PAD), F32),
        pltpu.VMEM((tt, GW), F32),
        pltpu.VMEM((tt, GW), F32),
        pltpu.VMEM((tt, GW), F32),
        pltpu.VMEM((tt, RET_KW), F32),
        pltpu.VMEM((tt, RET_KW), F32),
        pltpu.VMEM((tt, D_MODEL), F32),
        pltpu.VMEM((256, 256), F32),
        pltpu.VMEM((256, 256), F32),
        pltpu.VMEM((256, 256), F32),
        pltpu.VMEM((256, 128), F32),
    ]
    return pl.pallas_call(
        _mixer_kernel,
        grid=(bsz, seq // tt),
        in_specs=in_specs,
        out_specs=pl.BlockSpec((1, tt, D_MODEL), lambda b, t: (b, t, 0)),
        out_shape=jax.ShapeDtypeStruct(x.shape, F32),
        scratch_shapes=scratch,
        compiler_params=pltpu.CompilerParams(
            dimension_semantics=("arbitrary", "arbitrary"), vmem_limit_bytes=VMEM_LIMIT),
        name="mixer",
    )(x, mod_l, w_in_p, w_out_b, wg, bg, lb_l, gain, cos, sin, *const_inputs)


def _ffn_kernel(x_ref, mod_ref, w_up_ref, cw_ref, cb_ref, w_down_ref, fg_ref, out_ref, u_ref, *, final):
    tt = x_ref.shape[1]
    first = pl.program_id(1) == 0

    mod = mod_ref[0]
    sh2 = mod[:, 3 * D_MODEL:4 * D_MODEL]
    sc2 = mod[:, 4 * D_MODEL:5 * D_MODEL]
    g2 = mod[:, 5 * D_MODEL:6 * D_MODEL]

    @pl.when(first)
    def _():
        u_ref[0:8, :] = jnp.zeros((8, 2 * D_FF), F32)

    @pl.when(jnp.logical_not(first))
    def _():
        u_ref[0:8, :] = u_ref[tt:tt + 8, :]

    x = x_ref[0]
    h = (_rms(x) * (1.0 + sc2) + sh2).astype(BF16)
    cblk = 512
    for c0 in range(0, 2 * D_FF, cblk):
        u_ref[8:8 + tt, c0:c0 + cblk] = _dot(h, w_up_ref[:, c0:c0 + cblk])

    acc = jnp.zeros((tt, D_MODEL), F32)
    fblk = 256
    for c0 in range(0, D_FF, fblk):
        def conv(col):
            w = cw_ref[:, col:col + fblk]
            return (u_ref[6:6 + tt, col:col + fblk] * w[0:1, :]
                    + u_ref[7:7 + tt, col:col + fblk] * w[1:2, :]
                    + u_ref[8:8 + tt, col:col + fblk] * w[2:3, :]
                    + cb_ref[:, col:col + fblk])
        val = conv(c0)
        gate = conv(D_FF + c0)
        act = (_silu(gate) * val).astype(BF16)
        acc = acc + _dot(act, w_down_ref[c0:c0 + fblk, :])
    y = x + g2 * acc
    if final:
        y = _rms(y) * fg_ref[...]
    out_ref[0] = y


def _ffn(x, mod_l, w_up_b, conv_w, conv_b, w_down_b, final_gain, final):
    bsz, seq, _ = x.shape
    tt = TT_FFN
    return pl.pallas_call(
        functools.partial(_ffn_kernel, final=final),
        grid=(bsz, seq // tt),
        in_specs=[
            pl.BlockSpec((1, tt, D_MODEL), lambda b, t: (b, t, 0)),
            pl.BlockSpec((1, 1, 6 * D_MODEL), lambda b, t: (b, 0, 0)),
            _const_spec(w_up_b.shape),
            _const_spec(conv_w.shape),
            _const_spec(conv_b.shape),
            _const_spec(w_down_b.shape),
            _const_spec(final_gain.shape),
        ],
        out_specs=pl.BlockSpec((1, tt, D_MODEL), lambda b, t: (b, t, 0)),
        out_shape=jax.ShapeDtypeStruct(x.shape, F32),
        scratch_shapes=[pltpu.VMEM((tt + 8, 2 * D_FF), F32)],
        compiler_params=pltpu.CompilerParams(
            dimension_semantics=("arbitrary", "arbitrary"), vmem_limit_bytes=VMEM_LIMIT),
        name="ffn",
    )(x, mod_l, w_up_b, conv_w, conv_b, w_down_b, final_gain)


def kernel(x, c, w_in, w_gla_up, b_gla, lb_logits, head_gain, w_out, w_ada, b_ada,
           w_up, conv_w, conv_b, w_down, final_gain):
    bsz, seq, _ = x.shape
    m3, lvl = _level_constants()
    bd64, bdg, bdr = _block_masks()
    cos, sin = _rotary_tables(seq)
    dmat, qdec, kdec, cdec = _retention_tables()
    consts = (cos, sin, jnp.asarray(m3, BF16), jnp.asarray(lvl, F32),
              jnp.asarray(bd64, BF16), jnp.asarray(bdg, BF16), jnp.asarray(bdr, BF16),
              jnp.asarray(bd64, F32), jnp.asarray(bdg, F32), jnp.asarray(bdr, F32),
              dmat, qdec, kdec, cdec)

    mod, lb_all = _prep(c, w_ada, b_ada, lb_logits)
    wg_pad = jnp.zeros((DEPTH, 128, GLA_KW), F32).at[:, :GLA_RANK, :].set(w_gla_up).astype(BF16)
    fg = final_gain.reshape(1, D_MODEL)

    for l in range(DEPTH):
        mod_l = mod[l].reshape(bsz, 1, 6 * D_MODEL)
        x = _mixer(x, mod_l, _pack_w_in(w_in[l]), w_out[l].astype(BF16), wg_pad[l],
                   b_gla[l].reshape(1, GLA_KW), lb_all[l], head_gain[l].reshape(1, D_MODEL), consts)
        x = _ffn(x, mod_l, w_up[l].astype(BF16), conv_w[l], conv_b[l].reshape(1, 2 * D_FF),
                 w_down[l].astype(BF16), fg, final=(l == DEPTH - 1))
    return x
```

```python
import functools

import numpy as np
import jax
import jax.numpy as jnp
from jax import lax
from jax.experimental import pallas as pl
from jax.experimental.pallas import tpu as pltpu

F32 = jnp.float32
BF16 = jnp.bfloat16

D_MODEL = 1024
DEPTH = 2
CHUNK = 64
EPS = 1e-6
HEAD_DV = 64
RET_HEADS, HG_HEADS, GLA_HEADS = 4, 8, 4
RET_W, HG_W, GLA_W = 256, 512, 256
RET_KW, HG_FD, GLA_KW = 256, 512, 128
RET_DK, GLA_DK = 64, 32
GLA_RANK = 16
GLA_GATE_NORM = 16.0
ROPE_BASE = 10000.0
D_FF = 2816
CONV_W = 3

O_RQ, O_RK, O_RV = 0, 256, 512
O_HQ, O_HF, O_HI = 768, 1280, 1792
O_AQ, O_AK, O_AV, O_AL = 2304, 2432, 2560, 2816
PW = 2944
N_IN_PAD = 3968
GW = HG_FD + GLA_KW

SUB_BLOCKS = (16, 4, 1)
N_QV = 3
MM_ROWS = 64 * (4 + 4 + 3 + 1)

TT_MIX = 512
TT_FFN = 512
VMEM_LIMIT = 56 * 1024 * 1024


def _dot(a, b):
    return jnp.dot(a, b, preferred_element_type=F32)


def _dot_nt(a, b):
    return lax.dot_general(a, b, (((1,), (1,)), ((), ())), preferred_element_type=F32)


def _dot_tn(a, b):
    return lax.dot_general(a, b, (((0,), (0,)), ((), ())), preferred_element_type=F32)


def _sigmoid(x):
    return 1.0 / (1.0 + jnp.exp(-x))


def _silu(x):
    return x * _sigmoid(x)


def _log_sigmoid(x):
    return jnp.minimum(x, 0.0) - jnp.log1p(jnp.exp(-jnp.abs(x)))


def _rms(x):
    return x * lax.rsqrt(jnp.mean(x * x, axis=-1, keepdims=True) + EPS)


def _split3(x):
    hi = x.astype(BF16)
    r1 = x - hi.astype(F32)
    mid = r1.astype(BF16)
    lo = (r1 - mid.astype(F32)).astype(BF16)
    return hi, mid, lo


def _split2(x):
    hi = x.astype(BF16)
    lo = (x - hi.astype(F32)).astype(BF16)
    return hi, lo


def _level_constants():
    c = CHUNK
    ii, jj = np.meshgrid(np.arange(c), np.arange(c), indexing="ij")
    mats, masks = [], []
    for s in SUB_BLOCKS:
        sup = 4 * s
        for b in range(N_QV):
            mk = ((ii // sup) == (jj // sup)) & (((jj % sup) // s) == b) & (((ii % sup) // s) > b)
            masks.append(np.tile(mk.astype(np.float32), (1, 4)))
        if s > 1:
            m = np.zeros((c, c), np.float32)
            for j in range(c):
                b = (j % sup) // s
                if b < N_QV:
                    end = (j // sup) * sup + (b + 1) * s - 1
                    m[j, j + 1:end + 1] = 1.0
            mats.append(m)
        for b in range(N_QV):
            m = np.zeros((c, c), np.float32)
            for i in range(c):
                if (i % sup) // s > b:
                    end = (i // sup) * sup + (b + 1) * s - 1
                    m[i, end + 1:i + 1] = 1.0
            mats.append(m)
    mats.append(np.tril(np.ones((c, c), np.float32)))
    mstack = np.concatenate(mats, axis=0)
    assert mstack.shape[0] == MM_ROWS
    m3 = np.tile(mstack, (1, 3))
    return m3, np.stack(masks)


def _block_masks():
    r = np.arange(256)[:, None] // 64
    bd64 = (r == (np.arange(256)[None, :] // 64)).astype(np.float32)
    bdg = (r == (np.arange(128)[None, :] // 32)).astype(np.float32)
    bdr = (r == ((np.arange(256)[None, :] % 128) // 32)).astype(np.float32)
    return bd64, bdg, bdr


def _retention_tables():
    h = jnp.arange(RET_HEADS, dtype=F32)
    lg = jnp.log1p(-(2.0 ** (-5.0 - h)))
    idx = jnp.arange(CHUNK, dtype=F32)
    rel = idx[:, None] - idx[None, :]
    decay = jnp.where(rel >= 0, jnp.exp(lg[:, None, None] * jnp.maximum(rel, 0.0)), 0.0)
    dmat = jnp.transpose(decay, (1, 0, 2)).reshape(CHUNK, RET_HEADS * CHUNK)
    lane_head = (jnp.arange(256) % 128) // 32
    lg_lane = lg[lane_head]
    qdec = jnp.exp(lg_lane[None, :] * (idx[:, None] + 1.0))
    kdec = jnp.exp(lg_lane[None, :] * (CHUNK - 1 - idx[:, None]))
    cdec = jnp.exp(lg_lane * CHUNK)[None, :]
    return dmat, qdec, kdec, cdec


def _rotary_tables(seq):
    half = RET_DK // 2
    inv = ROPE_BASE ** (-jnp.arange(0, RET_DK, 2, dtype=F32) / RET_DK)
    ang = jnp.arange(seq, dtype=jnp.int32).astype(F32)[:, None] * inv[None, :]
    cos = jnp.tile(jnp.cos(ang), (1, 128 // half))
    sin = jnp.tile(jnp.sin(ang), (1, 128 // half))
    return cos, sin


def _pack_w_in(w):
    def rot_perm(block):
        b = block.reshape(D_MODEL, RET_HEADS, 2, RET_DK // 2)
        return jnp.transpose(b, (0, 2, 1, 3)).reshape(D_MODEL, RET_KW)
    rq = rot_perm(w[:, 0:256])
    rk = rot_perm(w[:, 256:512])
    rv, rg = w[:, 512:768], w[:, 768:1024]
    hq_hf_hi, hg = w[:, 1024:2560], w[:, 2560:3072]
    aq_ak_av, ag, al = w[:, 3072:3584], w[:, 3584:3840], w[:, 3840:3856]
    pad = jnp.zeros((D_MODEL, 128 - GLA_RANK), w.dtype)
    return jnp.concatenate([rq, rk, rv, hq_hf_hi, aq_ak_av, al, pad, rg, hg, ag], axis=1).astype(BF16)


def _prep_kernel(c_ref, w_ref, b_ref, lbl_ref, mod_ref, lb_ref):
    ca = _silu(c_ref[...])
    w = w_ref[0]
    c_hi, c_lo = _split2(ca)
    w_hi, w_lo = _split2(w)
    acc = _dot(c_hi, w_hi) + _dot(c_lo, w_hi) + _dot(c_hi, w_lo)
    mod_ref[0] = acc + b_ref[0]
    rows = [lbl_ref[l] for l in range(DEPTH)]
    mx = functools.reduce(jnp.maximum, rows)
    ex = [jnp.exp(r - mx) for r in rows]
    den = functools.reduce(lambda a, b: a + b, ex)
    run = None
    first = None
    for l in range(DEPTH):
        sm = ex[l] / den
        run = sm if run is None else run + sm
        if first is None:
            first = run
        lb_ref[l] = run - first


def _prep(c, w_ada, b_ada, lb_logits):
    b = c.shape[0]
    nblk = 6 * D_MODEL // 1024
    return pl.pallas_call(
        _prep_kernel,
        grid=(DEPTH, nblk),
        in_specs=[
            pl.BlockSpec((b, D_MODEL), lambda l, j: (0, 0)),
            pl.BlockSpec((1, D_MODEL, 1024), lambda l, j: (l, 0, j)),
            pl.BlockSpec((1, 1, 1024), lambda l, j: (l, 0, j)),
            pl.BlockSpec((DEPTH, 1, HG_FD), lambda l, j: (0, 0, 0)),
        ],
        out_specs=[
            pl.BlockSpec((1, b, 1024), lambda l, j: (l, 0, j)),
            pl.BlockSpec((DEPTH, 1, HG_FD), lambda l, j: (0, 0, 0)),
        ],
        out_shape=[
            jax.ShapeDtypeStruct((DEPTH, b, 6 * D_MODEL), F32),
            jax.ShapeDtypeStruct((DEPTH, 1, HG_FD), F32),
        ],
        compiler_params=pltpu.CompilerParams(dimension_semantics=("arbitrary", "arbitrary")),
        name="prep",
    )(c, w_ada, b_ada.reshape(DEPTH, 1, 6 * D_MODEL), lb_logits.reshape(DEPTH, 1, HG_FD))


def _mixer_kernel(x_ref, mod_ref, w_in_ref, w_out_ref, wg_ref, bg_ref, lb_ref, gain_ref,
                  cos_ref, sin_ref, m3_ref, lvl_ref, bd64b_ref, bdgb_ref, bdrb_ref, bdgt_ref,
                  bd64f_ref, bdgf_ref, bdrf_ref, dmat_ref, qdec_ref, kdec_ref, cdec_ref,
                  out_ref,
                  proj_ref, sg_ref, lf_ref, qg_ref, kg_ref, qr_ref, kr_ref, o_ref,
                  st_r_ref, st_h0_ref, st_h1_ref, st_g_ref):
    tt = x_ref.shape[1]

    @pl.when(pl.program_id(1) == 0)
    def _():
        st_r_ref[...] = jnp.zeros_like(st_r_ref)
        st_h0_ref[...] = jnp.zeros_like(st_h0_ref)
        st_h1_ref[...] = jnp.zeros_like(st_h1_ref)
        st_g_ref[...] = jnp.zeros_like(st_g_ref)

    mod = mod_ref[0]
    sh1 = mod[:, 0:D_MODEL]
    sc1 = mod[:, D_MODEL:2 * D_MODEL]
    g1 = mod[:, 2 * D_MODEL:3 * D_MODEL]
    gain = gain_ref[...]

    h = (_rms(x_ref[0]) * (1.0 + sc1) + sh1).astype(BF16)

    def inproj(c0, c1):
        return _dot(h, w_in_ref[:, c0:c1])

    for c0 in range(PW, N_IN_PAD, 512):
        cols = slice(c0 - PW, c0 - PW + 512)
        sg_ref[:, cols] = _silu(inproj(c0, c0 + 512)) * gain[:, cols]

    lb = lb_ref[0]
    qg_ref[:, 0:HG_FD] = _silu(inproj(O_HQ, O_HQ + HG_FD))
    z = inproj(O_HF, O_HF + HG_FD)
    a = jnp.log1p(-lb) + _log_sigmoid(z)
    lb_pos = lb > 0.0
    b = jnp.log(jnp.where(lb_pos, lb, 1.0))
    both = jnp.maximum(a, b) + jnp.log1p(jnp.exp(-jnp.abs(a - b)))
    lf_ref[:, 0:HG_FD] = jnp.where(lb_pos, both, a)
    kg_ref[:, 0:HG_FD] = (1.0 - lb) * _sigmoid(-z)

    rqk = inproj(O_RQ, O_RQ + 512)
    cos = cos_ref[...]
    sin = sin_ref[...]
    q1, q2, k1, k2 = rqk[:, 0:128], rqk[:, 128:256], rqk[:, 256:384], rqk[:, 384:512]
    kscale = RET_DK ** -0.5
    qr_ref[:, 0:128] = q1 * cos - q2 * sin
    qr_ref[:, 128:256] = q1 * sin + q2 * cos
    kr_ref[:, 0:128] = (k1 * cos - k2 * sin) * kscale
    kr_ref[:, 128:256] = (k1 * sin + k2 * cos) * kscale

    aqk = inproj(O_AQ, O_AQ + 256)
    qg_ref[:, HG_FD:GW] = aqk[:, 0:GLA_KW]
    kg_ref[:, HG_FD:GW] = aqk[:, GLA_KW:2 * GLA_KW] * (GLA_DK ** -0.5)
    alow = inproj(O_AL, O_AL + 128).astype(BF16)
    glogit = _dot(alow, wg_ref[...]) + bg_ref[...]
    lf_ref[:, HG_FD:GW] = _log_sigmoid(glogit) * (1.0 / GLA_GATE_NORM)

    proj_ref[:, O_RV:O_RV + RET_W] = inproj(O_RV, O_RV + RET_W)
    proj_ref[:, O_HI:O_HI + HG_W] = inproj(O_HI, O_HI + HG_W)
    proj_ref[:, O_AV:O_AV + GLA_W] = inproj(O_AV, O_AV + GLA_W)

    bd64b = bd64b_ref[...]
    bdgb = bdgb_ref[...]
    bdrb = bdrb_ref[...]
    bdgt = bdgt_ref[...]

    groups = (
        (0, 256, O_HI, 256, st_h0_ref, bd64b, bd64b, bd64f_ref),
        (256, 512, O_HI + 256, 512, st_h1_ref, bd64b, bd64b, bd64f_ref),
        (512, 640, O_AV, 768, st_g_ref, bdgb, bdgt, bdgf_ref),
    )

    def chunk_body(ci, carry):
        r0 = pl.multiple_of(ci * CHUNK, CHUNK)
        rows = pl.ds(r0, CHUNK)

        qr = qr_ref[rows, :]
        kr = kr_ref[rows, :]
        vr = proj_ref[rows, O_RV:O_RV + RET_W].astype(BF16)
        qrb = qr.astype(BF16)
        krb = kr.astype(BF16)
        kst = jnp.concatenate([krb] * 4, axis=0) * bdrb
        p = _dot_nt(qrb, kst) * dmat_ref[...]
        vst = jnp.concatenate([vr] * 4, axis=0) * bd64b
        o_r = _dot(p.astype(BF16), vst)
        st = st_r_ref[...]
        o_r = o_r + _dot_nt((qr * qdec_ref[...]).astype(BF16), st.astype(BF16))
        kv = _dot_tn(vr, (kr * kdec_ref[...]).astype(BF16))
        st_r_ref[...] = st * cdec_ref[...] + kv * bdrf_ref[...]
        o_ref[rows, 0:RET_W] = o_r

        lf = lf_ref[rows, :]
        qg = qg_ref[rows, :]
        kg = kg_ref[rows, :]
        hi, mid, lo = _split3(lf)
        gmat = _dot(m3_ref[...], jnp.concatenate([hi, mid, lo], axis=0))
        acc = [jnp.zeros((CHUNK, 256), F32) for _ in groups]
        cum = gmat[MM_ROWS - CHUNK:MM_ROWS, :]
        off = 0
        for li, s in enumerate(SUB_BLOCKS):
            if s > 1:
                zk = (kg * jnp.exp(gmat[off:off + CHUNK, :])).astype(BF16)
                off += CHUNK
            else:
                zk = kg.astype(BF16)
            qexp = [gmat[off + b * CHUNK:off + (b + 1) * CHUNK, :] for b in range(N_QV)]
            off += N_QV * CHUNK
            zq = jnp.concatenate([(qg * jnp.exp(e)).astype(BF16) for e in qexp], axis=0)
            for gi, (l0, l1, _, _, _, bdk, _, _) in enumerate(groups):
                kst = jnp.concatenate([zk[:, l0:l1]] * 4, axis=0) * bdk
                sc = _dot_nt(zq[:, l0:l1], kst)
                for b in range(N_QV):
                    acc[gi] = acc[gi] + sc[b * CHUNK:(b + 1) * CHUNK, :] * lvl_ref[N_QV * li + b]
        last = cum[CHUNK - 1:CHUNK, :]
        qc = (qg * jnp.exp(cum)).astype(BF16)
        kc = (kg * jnp.exp(last - cum)).astype(BF16)
        elast = jnp.exp(last)
        qk = (qg * kg).astype(BF16)
        for gi, (l0, l1, vcol, ocol, st_ref, _, ones, bdf_ref) in enumerate(groups):
            v = proj_ref[rows, vcol:vcol + 256]
            vb = v.astype(BF16)
            vst = jnp.concatenate([vb] * 4, axis=0) * bd64b
            o_g = _dot(acc[gi].astype(BF16), vst)
            o_g = o_g + _dot(qk[:, l0:l1], ones) * v
            st = st_ref[...]
            o_g = o_g + _dot_nt(qc[:, l0:l1], st.astype(BF16))
            kv = _dot_tn(vb, kc[:, l0:l1])
            st_ref[...] = st * elast[:, l0:l1] + kv * bdf_ref[...]
            o_ref[rows, ocol:ocol + 256] = o_g
        return carry

    lax.fori_loop(0, tt // CHUNK, chunk_body, 0, unroll=True)

    ys = []
    for bi in range(4):
        cols = slice(bi * 256, (bi + 1) * 256)
        o = o_ref[:, cols]
        s_hi, s_lo = _split2(o * o)
        ms = (_dot(s_hi, bd64b) + _dot(s_lo, bd64b)) * (1.0 / HEAD_DV)
        ys.append((o * lax.rsqrt(ms + EPS) * sg_ref[:, cols]).astype(BF16))
    y = jnp.concatenate(ys, axis=1)
    out_ref[0] = x_ref[0] + g1 * _dot(y, w_out_ref[...])


def _const_spec(shape):
    nd = len(shape)
    return pl.BlockSpec(shape, lambda b, t, _nd=nd: (0,) * _nd, pipeline_mode=pl.Buffered(1))


def _mixer(x, mod_l, w_in_p, w_out_b, wg, bg, lb_l, gain, consts):
    bsz, seq, _ = x.shape
    tt = TT_MIX
    (cos, sin, m3, lvl, bd64b, bdgb, bdrb, bdgt, bd64f, bdgf, bdrf, dmat, qdec, kdec, cdec) = consts
    const_inputs = (m3, lvl, bd64b, bdgb, bdrb, bdgt, bd64f, bdgf, bdrf, dmat, qdec, kdec, cdec)
    in_specs = [
        pl.BlockSpec((1, tt, D_MODEL), lambda b, t: (b, t, 0)),
        pl.BlockSpec((1, 1, 6 * D_MODEL), lambda b, t: (b, 0, 0)),
        _const_spec(w_in_p.shape),
        _const_spec(w_out_b.shape),
        _const_spec(wg.shape),
        _const_spec(bg.shape),
        _const_spec(lb_l.shape),
        _const_spec(gain.shape),
        pl.BlockSpec((tt, 128), lambda b, t: (t, 0)),
        pl.BlockSpec((tt, 128), lambda b, t: (t, 0)),
    ] + [_const_spec(a.shape) for a in const_inputs]
    scratch = [
        pltpu.VMEM((tt, PW), F32),
        pltpu.VMEM((tt, D_MODEL), F32),
        pltpu.VMEM((tt, GW), F32),
        pltpu.VMEM((tt, GW), F32),
        pltpu.VMEM((tt, GW), F32),
        pltpu.VMEM((tt, RET_KW), F32),
        pltpu.VMEM((tt, RET_KW), F32),
        pltpu.VMEM((tt, D_MODEL), F32),
        pltpu.VMEM((256, 256), F32),
        pltpu.VMEM((256, 256), F32),
        pltpu.VMEM((256, 256), F32),
        pltpu.VMEM((256, 128), F32),
    ]
    return pl.pallas_call(
        _mixer_kernel,
        grid=(bsz, seq // tt),
        in_specs=in_specs,
        out_specs=pl.BlockSpec((1, tt, D_MODEL), lambda b, t: (b, t, 0)),
        out_shape=jax.ShapeDtypeStruct(x.shape, F32),
        scratch_shapes=scratch,
        compiler_params=pltpu.CompilerParams(
            dimension_semantics=("arbitrary", "arbitrary"), vmem_limit_bytes=VMEM_LIMIT),
        name="mixer",
    )(x, mod_l, w_in_p, w_out_b, wg, bg, lb_l, gain, cos, sin, *const_inputs)


def _ffn_kernel(x_ref, mod_ref, w_up_ref, cw_ref, cb_ref, w_down_ref, fg_ref, out_ref, u_ref, *, final):
    tt = x_ref.shape[1]
    first = pl.program_id(1) == 0

    mod = mod_ref[0]
    sh2 = mod[:, 3 * D_MODEL:4 * D_MODEL]
    sc2 = mod[:, 4 * D_MODEL:5 * D_MODEL]
    g2 = mod[:, 5 * D_MODEL:6 * D_MODEL]

    @pl.when(first)
    def _():
        u_ref[0:8, :] = jnp.zeros((8, 2 * D_FF), F32)

    @pl.when(jnp.logical_not(first))
    def _():
        u_ref[0:8, :] = u_ref[tt:tt + 8, :]

    x = x_ref[0]
    h = (_rms(x) * (1.0 + sc2) + sh2).astype(BF16)
    cblk = 512
    for c0 in range(0, 2 * D_FF, cblk):
        u_ref[8:8 + tt, c0:c0 + cblk] = _dot(h, w_up_ref[:, c0:c0 + cblk])

    acc = jnp.zeros((tt, D_MODEL), F32)
    fblk = 256
    for c0 in range(0, D_FF, fblk):
        def conv(col):
            w = cw_ref[:, col:col + fblk]
            return (u_ref[6:6 + tt, col:col + fblk] * w[0:1, :]
                    + u_ref[7:7 + tt, col:col + fblk] * w[1:2, :]
                    + u_ref[8:8 + tt, col:col + fblk] * w[2:3, :]
                    + cb_ref[:, col:col + fblk])
        val = conv(c0)
        gate = conv(D_FF + c0)
        act = (_silu(gate) * val).astype(BF16)
        acc = acc + _dot(act, w_down_ref[c0:c0 + fblk, :])
    y = x + g2 * acc
    if final:
        y = _rms(y) * fg_ref[...]
    out_ref[0] = y


def _ffn(x, mod_l, w_up_b, conv_w, conv_b, w_down_b, final_gain, final):
    bsz, seq, _ = x.shape
    tt = TT_FFN
    return pl.pallas_call(
        functools.partial(_ffn_kernel, final=final),
        grid=(bsz, seq // tt),
        in_specs=[
            pl.BlockSpec((1, tt, D_MODEL), lambda b, t: (b, t, 0)),
            pl.BlockSpec((1, 1, 6 * D_MODEL), lambda b, t: (b, 0, 0)),
            _const_spec(w_up_b.shape),
            _const_spec(conv_w.shape),
            _const_spec(conv_b.shape),
            _const_spec(w_down_b.shape),
            _const_spec(final_gain.shape),
        ],
        out_specs=pl.BlockSpec((1, tt, D_MODEL), lambda b, t: (b, t, 0)),
        out_shape=jax.ShapeDtypeStruct(x.shape, F32),
        scratch_shapes=[pltpu.VMEM((tt + 8, 2 * D_FF), F32)],
        compiler_params=pltpu.CompilerParams(
            dimension_semantics=("arbitrary", "arbitrary"), vmem_limit_bytes=VMEM_LIMIT),
        name="ffn",
    )(x, mod_l, w_up_b, conv_w, conv_b, w_down_b, final_gain)


def kernel(x, c, w_in, w_gla_up, b_gla, lb_logits, head_gain, w_out, w_ada, b_ada,
           w_up, conv_w, conv_b, w_down, final_gain):
    bsz, seq, _ = x.shape
    m3, lvl = _level_constants()
    bd64, bdg, bdr = _block_masks()
    cos, sin = _rotary_tables(seq)
    dmat, qdec, kdec, cdec = _retention_tables()
    consts = (cos, sin, jnp.asarray(m3, BF16), jnp.asarray(lvl, F32),
              jnp.asarray(bd64, BF16), jnp.asarray(bdg, BF16), jnp.asarray(bdr, BF16),
              jnp.asarray(bdg.T, BF16),
              jnp.asarray(bd64, F32), jnp.asarray(bdg, F32), jnp.asarray(bdr, F32),
              dmat, qdec, kdec, cdec)

    mod, lb_all = _prep(c, w_ada, b_ada, lb_logits)
    wg_pad = jnp.zeros((DEPTH, 128, GLA_KW), F32).at[:, :GLA_RANK, :].set(w_gla_up).astype(BF16)
    fg = final_gain.reshape(1, D_MODEL)

    for l in range(DEPTH):
        mod_l = mod[l].reshape(bsz, 1, 6 * D_MODEL)
        x = _mixer(x, mod_l, _pack_w_in(w_in[l]), w_out[l].astype(BF16), wg_pad[l],
                   b_gla[l].reshape(1, GLA_KW), lb_all[l], head_gain[l].reshape(1, D_MODEL), consts)
        x = _ffn(x, mod_l, w_up[l].astype(BF16), conv_w[l], conv_b[l].reshape(1, 2 * D_FF),
                 w_down[l].astype(BF16), fg, final=(l == DEPTH - 1))
    return x
```

```python
import functools

import numpy as np
import jax
import jax.numpy as jnp
from jax import lax
from jax.experimental import pallas as pl
from jax.experimental.pallas import tpu as pltpu

F32 = jnp.float32
BF16 = jnp.bfloat16

D_MODEL = 1024
DEPTH = 2
CHUNK = 64
EPS = 1e-6
HEAD_DV = 64
RET_HEADS, HG_HEADS, GLA_HEADS = 4, 8, 4
RET_W, HG_W, GLA_W = 256, 512, 256
RET_KW, HG_FD, GLA_KW = 256, 512, 128
RET_DK, GLA_DK = 64, 32
GLA_RANK = 16
GLA_GATE_NORM = 16.0
ROPE_BASE = 10000.0
D_FF = 2816
CONV_W = 3

O_RQ, O_RK, O_RV = 0, 256, 512
O_HQ, O_HF, O_HI = 768, 1280, 1792
O_AQ, O_AK, O_AV, O_AL = 2304, 2432, 2560, 2816
PW = 2944
N_IN_PAD = 3968
GW = HG_FD + GLA_KW

SUB_BLOCKS = (16, 4, 1)
N_QV = 3
MM_ROWS = 64 * (4 + 4 + 3 + 1)

TT_MIX = 512
TT_FFN = 512
VMEM_LIMIT = 56 * 1024 * 1024


def _dot(a, b):
    return jnp.dot(a, b, preferred_element_type=F32)


def _dot_nt(a, b):
    return lax.dot_general(a, b, (((1,), (1,)), ((), ())), preferred_element_type=F32)


def _dot_tn(a, b):
    return lax.dot_general(a, b, (((0,), (0,)), ((), ())), preferred_element_type=F32)


def _sigmoid(x):
    return 1.0 / (1.0 + jnp.exp(-x))


def _silu(x):
    return x * _sigmoid(x)


def _log_sigmoid(x):
    return jnp.minimum(x, 0.0) - jnp.log1p(jnp.exp(-jnp.abs(x)))


def _rms(x):
    return x * lax.rsqrt(jnp.mean(x * x, axis=-1, keepdims=True) + EPS)


def _split2(x):
    hi = x.astype(BF16)
    lo = (x - hi.astype(F32)).astype(BF16)
    return hi, lo


def _level_constants():
    c = CHUNK
    ii, jj = np.meshgrid(np.arange(c), np.arange(c), indexing="ij")
    mats, masks = [], []
    for s in SUB_BLOCKS:
        sup = 4 * s
        for b in range(N_QV):
            mk = ((ii // sup) == (jj // sup)) & (((jj % sup) // s) == b) & (((ii % sup) // s) > b)
            masks.append(np.tile(mk.astype(np.float32), (1, 4)))
        if s > 1:
            m = np.zeros((c, c), np.float32)
            for j in range(c):
                b = (j % sup) // s
                if b < N_QV:
                    end = (j // sup) * sup + (b + 1) * s - 1
                    m[j, j + 1:end + 1] = 1.0
            mats.append(m)
        for b in range(N_QV):
            m = np.zeros((c, c), np.float32)
            for i in range(c):
                if (i % sup) // s > b:
                    end = (i // sup) * sup + (b + 1) * s - 1
                    m[i, end + 1:i + 1] = 1.0
            mats.append(m)
    mats.append(np.tril(np.ones((c, c), np.float32)))
    mstack = np.concatenate(mats, axis=0)
    assert mstack.shape[0] == MM_ROWS
    m3 = np.tile(mstack, (1, 2))
    return m3, np.stack(masks)


def _block_masks():
    r = np.arange(256)[:, None] // 64
    bd64 = (r == (np.arange(256)[None, :] // 64)).astype(np.float32)
    bdg = (r == (np.arange(128)[None, :] // 32)).astype(np.float32)
    bdr = (r == ((np.arange(256)[None, :] % 128) // 32)).astype(np.float32)
    return bd64, bdg, bdr


def _retention_tables():
    h = jnp.arange(RET_HEADS, dtype=F32)
    lg = jnp.log1p(-(2.0 ** (-5.0 - h)))
    idx = jnp.arange(CHUNK, dtype=F32)
    rel = idx[:, None] - idx[None, :]
    decay = jnp.where(rel >= 0, jnp.exp(lg[:, None, None] * jnp.maximum(rel, 0.0)), 0.0)
    dmat = jnp.transpose(decay, (1, 0, 2)).reshape(CHUNK, RET_HEADS * CHUNK)
    lane_head = (jnp.arange(256) % 128) // 32
    lg_lane = lg[lane_head]
    qdec = jnp.exp(lg_lane[None, :] * (idx[:, None] + 1.0))
    kdec = jnp.exp(lg_lane[None, :] * (CHUNK - 1 - idx[:, None]))
    cdec = jnp.exp(lg_lane * CHUNK)[None, :]
    return dmat, qdec, kdec, cdec


def _rotary_tables(seq):
    half = RET_DK // 2
    inv = ROPE_BASE ** (-jnp.arange(0, RET_DK, 2, dtype=F32) / RET_DK)
    ang = jnp.arange(seq, dtype=jnp.int32).astype(F32)[:, None] * inv[None, :]
    cos = jnp.tile(jnp.cos(ang), (1, 128 // half))
    sin = jnp.tile(jnp.sin(ang), (1, 128 // half))
    return cos, sin


def _pack_w_in(w):
    def rot_perm(block):
        b = block.reshape(D_MODEL, RET_HEADS, 2, RET_DK // 2)
        return jnp.transpose(b, (0, 2, 1, 3)).reshape(D_MODEL, RET_KW)
    rq = rot_perm(w[:, 0:256])
    rk = rot_perm(w[:, 256:512])
    rv, rg = w[:, 512:768], w[:, 768:1024]
    hq_hf_hi, hg = w[:, 1024:2560], w[:, 2560:3072]
    aq_ak_av, ag, al = w[:, 3072:3584], w[:, 3584:3840], w[:, 3840:3856]
    pad = jnp.zeros((D_MODEL, 128 - GLA_RANK), w.dtype)
    return jnp.concatenate([rq, rk, rv, hq_hf_hi, aq_ak_av, al, pad, rg, hg, ag], axis=1).astype(BF16)


def _prep_kernel(c_ref, w_ref, b_ref, lbl_ref, mod_ref, lb_ref):
    ca = _silu(c_ref[...])
    w = w_ref[0]
    c_hi, c_lo = _split2(ca)
    w_hi, w_lo = _split2(w)
    acc = _dot(c_hi, w_hi) + _dot(c_lo, w_hi) + _dot(c_hi, w_lo)
    mod_ref[0] = acc + b_ref[0]
    rows = [lbl_ref[l] for l in range(DEPTH)]
    mx = functools.reduce(jnp.maximum, rows)
    ex = [jnp.exp(r - mx) for r in rows]
    den = functools.reduce(lambda a, b: a + b, ex)
    run = None
    first = None
    for l in range(DEPTH):
        sm = ex[l] / den
        run = sm if run is None else run + sm
        if first is None:
            first = run
        lb_ref[l] = run - first


def _prep(c, w_ada, b_ada, lb_logits):
    b = c.shape[0]
    nblk = 6 * D_MODEL // 1024
    return pl.pallas_call(
        _prep_kernel,
        grid=(DEPTH, nblk),
        in_specs=[
            pl.BlockSpec((b, D_MODEL), lambda l, j: (0, 0)),
            pl.BlockSpec((1, D_MODEL, 1024), lambda l, j: (l, 0, j)),
            pl.BlockSpec((1, 1, 1024), lambda l, j: (l, 0, j)),
            pl.BlockSpec((DEPTH, 1, HG_FD), lambda l, j: (0, 0, 0)),
        ],
        out_specs=[
            pl.BlockSpec((1, b, 1024), lambda l, j: (l, 0, j)),
            pl.BlockSpec((DEPTH, 1, HG_FD), lambda l, j: (0, 0, 0)),
        ],
        out_shape=[
            jax.ShapeDtypeStruct((DEPTH, b, 6 * D_MODEL), F32),
            jax.ShapeDtypeStruct((DEPTH, 1, HG_FD), F32),
        ],
        compiler_params=pltpu.CompilerParams(dimension_semantics=("arbitrary", "arbitrary")),
        name="prep",
    )(c, w_ada, b_ada.reshape(DEPTH, 1, 6 * D_MODEL), lb_logits.reshape(DEPTH, 1, HG_FD))


def _mixer_kernel(x_ref, mod_ref, w_in_ref, w_out_ref, wg_ref, bg_ref, lb_ref, gain_ref,
                  cos_ref, sin_ref, m3_ref, lvl_ref, bd64b_ref, bdgb_ref, bdrb_ref, bdgt_ref,
                  bd64f_ref, bdgf_ref, bdrf_ref, dmat_ref, qdec_ref, kdec_ref, cdec_ref,
                  out_ref,
                  proj_ref, sg_ref, lf_ref, qg_ref, kg_ref, qr_ref, kr_ref, o_ref,
                  st_r_ref, st_h0_ref, st_h1_ref, st_g_ref):
    tt = x_ref.shape[1]

    @pl.when(pl.program_id(1) == 0)
    def _():
        st_r_ref[...] = jnp.zeros_like(st_r_ref)
        st_h0_ref[...] = jnp.zeros_like(st_h0_ref)
        st_h1_ref[...] = jnp.zeros_like(st_h1_ref)
        st_g_ref[...] = jnp.zeros_like(st_g_ref)

    mod = mod_ref[0]
    sh1 = mod[:, 0:D_MODEL]
    sc1 = mod[:, D_MODEL:2 * D_MODEL]
    g1 = mod[:, 2 * D_MODEL:3 * D_MODEL]
    gain = gain_ref[...]

    h = (_rms(x_ref[0]) * (1.0 + sc1) + sh1).astype(BF16)

    def inproj(c0, c1):
        return _dot(h, w_in_ref[:, c0:c1])

    def store_log_gates(cols, lf):
        lf_ref[:, cols] = lf

    for c0 in range(PW, N_IN_PAD, 512):
        cols = slice(c0 - PW, c0 - PW + 512)
        sg_ref[:, cols] = _silu(inproj(c0, c0 + 512)) * gain[:, cols]

    lb = lb_ref[0]
    qg_ref[:, 0:HG_FD] = _silu(inproj(O_HQ, O_HQ + HG_FD))
    z = inproj(O_HF, O_HF + HG_FD)
    a = jnp.log1p(-lb) + _log_sigmoid(z)
    lb_pos = lb > 0.0
    b = jnp.log(jnp.where(lb_pos, lb, 1.0))
    both = jnp.maximum(a, b) + jnp.log1p(jnp.exp(-jnp.abs(a - b)))
    store_log_gates(slice(0, HG_FD), jnp.where(lb_pos, both, a))
    kg_ref[:, 0:HG_FD] = (1.0 - lb) * _sigmoid(-z)

    rqk = inproj(O_RQ, O_RQ + 512)
    cos = cos_ref[...]
    sin = sin_ref[...]
    q1, q2, k1, k2 = rqk[:, 0:128], rqk[:, 128:256], rqk[:, 256:384], rqk[:, 384:512]
    kscale = RET_DK ** -0.5
    qr_ref[:, 0:128] = q1 * cos - q2 * sin
    qr_ref[:, 128:256] = q1 * sin + q2 * cos
    kr_ref[:, 0:128] = (k1 * cos - k2 * sin) * kscale
    kr_ref[:, 128:256] = (k1 * sin + k2 * cos) * kscale

    aqk = inproj(O_AQ, O_AQ + 256)
    qg_ref[:, HG_FD:GW] = aqk[:, 0:GLA_KW]
    kg_ref[:, HG_FD:GW] = aqk[:, GLA_KW:2 * GLA_KW] * (GLA_DK ** -0.5)
    alow = inproj(O_AL, O_AL + 128).astype(BF16)
    glogit = _dot(alow, wg_ref[...]) + bg_ref[...]
    store_log_gates(slice(HG_FD, GW), _log_sigmoid(glogit) * (1.0 / GLA_GATE_NORM))

    proj_ref[:, O_RV:O_RV + RET_W] = inproj(O_RV, O_RV + RET_W)
    proj_ref[:, O_HI:O_HI + HG_W] = inproj(O_HI, O_HI + HG_W)
    proj_ref[:, O_AV:O_AV + GLA_W] = inproj(O_AV, O_AV + GLA_W)

    bd64b = bd64b_ref[...]
    bdgb = bdgb_ref[...]
    bdrb = bdrb_ref[...]
    bdgt = bdgt_ref[...]

    groups = (
        (0, 256, O_HI, 256, st_h0_ref, bd64b, bd64b, bd64f_ref),
        (256, 512, O_HI + 256, 512, st_h1_ref, bd64b, bd64b, bd64f_ref),
        (512, 640, O_AV, 768, st_g_ref, bdgb, bdgt, bdgf_ref),
    )

    def chunk_body(ci, carry):
        r0 = pl.multiple_of(ci * CHUNK, CHUNK)
        rows = pl.ds(r0, CHUNK)

        qr = qr_ref[rows, :]
        kr = kr_ref[rows, :]
        vr = proj_ref[rows, O_RV:O_RV + RET_W].astype(BF16)
        qrb = qr.astype(BF16)
        krb = kr.astype(BF16)
        kst = jnp.concatenate([krb] * 4, axis=0) * bdrb
        p = _dot_nt(qrb, kst) * dmat_ref[...]
        vst = jnp.concatenate([vr] * 4, axis=0) * bd64b
        o_r = _dot(p.astype(BF16), vst)
        st = st_r_ref[...]
        o_r = o_r + _dot_nt((qr * qdec_ref[...]).astype(BF16), st.astype(BF16))
        kv = _dot_tn(vr, (kr * kdec_ref[...]).astype(BF16))
        st_r_ref[...] = st * cdec_ref[...] + kv * bdrf_ref[...]
        o_ref[rows, 0:RET_W] = o_r

        qg = qg_ref[rows, :]
        kg = kg_ref[rows, :]
        hi, lo = _split2(lf_ref[rows, :])
        gmat = _dot(m3_ref[...], jnp.concatenate([hi, lo], axis=0))
        acc = [jnp.zeros((CHUNK, 256), F32) for _ in groups]
        cum = gmat[MM_ROWS - CHUNK:MM_ROWS, :]
        off = 0
        for li, s in enumerate(SUB_BLOCKS):
            if s > 1:
                zk = (kg * jnp.exp(gmat[off:off + CHUNK, :])).astype(BF16)
                off += CHUNK
            else:
                zk = kg.astype(BF16)
            qexp = [gmat[off + b * CHUNK:off + (b + 1) * CHUNK, :] for b in range(N_QV)]
            off += N_QV * CHUNK
            zq = jnp.concatenate([(qg * jnp.exp(e)).astype(BF16) for e in qexp], axis=0)
            for gi, (l0, l1, _, _, _, bdk, _, _) in enumerate(groups):
                kst = jnp.concatenate([zk[:, l0:l1]] * 4, axis=0) * bdk
                sc = _dot_nt(zq[:, l0:l1], kst)
                for b in range(N_QV):
                    acc[gi] = acc[gi] + sc[b * CHUNK:(b + 1) * CHUNK, :] * lvl_ref[N_QV * li + b]
        last = cum[CHUNK - 1:CHUNK, :]
        qc = (qg * jnp.exp(cum)).astype(BF16)
        kc = (kg * jnp.exp(last - cum)).astype(BF16)
        elast = jnp.exp(last)
        qk = (qg * kg).astype(BF16)
        for gi, (l0, l1, vcol, ocol, st_ref, _, ones, bdf_ref) in enumerate(groups):
            v = proj_ref[rows, vcol:vcol + 256]
            vb = v.astype(BF16)
            vst = jnp.concatenate([vb] * 4, axis=0) * bd64b
            o_g = _dot(acc[gi].astype(BF16), vst)
            o_g = o_g + _dot(qk[:, l0:l1], ones) * v
            st = st_ref[...]
            o_g = o_g + _dot_nt(qc[:, l0:l1], st.astype(BF16))
            kv = _dot_tn(vb, kc[:, l0:l1])
            st_ref[...] = st * elast[:, l0:l1] + kv * bdf_ref[...]
            o_ref[rows, ocol:ocol + 256] = o_g
        return carry

    lax.fori_loop(0, tt // CHUNK, chunk_body, 0, unroll=True)

    ys = []
    for bi in range(4):
        cols = slice(bi * 256, (bi + 1) * 256)
        o = o_ref[:, cols]
        ms = _dot((o * o).astype(BF16), bd64b) * (1.0 / HEAD_DV)
        ys.append((o * lax.rsqrt(ms + EPS) * sg_ref[:, cols]).astype(BF16))
    y = jnp.concatenate(ys, axis=1)
    out_ref[0] = x_ref[0] + g1 * _dot(y, w_out_ref[...])


def _const_spec(shape):
    nd = len(shape)
    return pl.BlockSpec(shape, lambda b, t, _nd=nd: (0,) * _nd, pipeline_mode=pl.Buffered(1))


def _mixer(x, mod_l, w_in_p, w_out_b, wg, bg, lb_l, gain, consts):
    bsz, seq, _ = x.shape
    tt = TT_MIX
    (cos, sin, m3, lvl, bd64b, bdgb, bdrb, bdgt, bd64f, bdgf, bdrf, dmat, qdec, kdec, cdec) = consts
    const_inputs = (m3, lvl, bd64b, bdgb, bdrb, bdgt, bd64f, bdgf, bdrf, dmat, qdec, kdec, cdec)
    in_specs = [
        pl.BlockSpec((1, tt, D_MODEL), lambda b, t: (b, t, 0)),
        pl.BlockSpec((1, 1, 6 * D_MODEL), lambda b, t: (b, 0, 0)),
        _const_spec(w_in_p.shape),
        _const_spec(w_out_b.shape),
        _const_spec(wg.shape),
        _const_spec(bg.shape),
        _const_spec(lb_l.shape),
        _const_spec(gain.shape),
        pl.BlockSpec((tt, 128), lambda b, t: (t, 0)),
        pl.BlockSpec((tt, 128), lambda b, t: (t, 0)),
    ] + [_const_spec(a.shape) for a in const_inputs]
    scratch = [
        pltpu.VMEM((tt, PW), F32),
        pltpu.VMEM((tt, D_MODEL), F32),
        pltpu.VMEM((tt, GW), F32),
        pltpu.VMEM((tt, GW), F32),
        pltpu.VMEM((tt, GW), F32),
        pltpu.VMEM((tt, RET_KW), F32),
        pltpu.VMEM((tt, RET_KW), F32),
        pltpu.VMEM((tt, D_MODEL), F32),
        pltpu.VMEM((256, 256), F32),
        pltpu.VMEM((256, 256), F32),
        pltpu.VMEM((256, 256), F32),
        pltpu.VMEM((256, 128), F32),
    ]
    return pl.pallas_call(
        _mixer_kernel,
        grid=(bsz, seq // tt),
        in_specs=in_specs,
        out_specs=pl.BlockSpec((1, tt, D_MODEL), lambda b, t: (b, t, 0)),
        out_shape=jax.ShapeDtypeStruct(x.shape, F32),
        scratch_shapes=scratch,
        compiler_params=pltpu.CompilerParams(
            dimension_semantics=("arbitrary", "arbitrary"), vmem_limit_bytes=VMEM_LIMIT),
        name="mixer",
    )(x, mod_l, w_in_p, w_out_b, wg, bg, lb_l, gain, cos, sin, *const_inputs)


def _ffn_kernel(x_ref, mod_ref, w_up_ref, cw_ref, cb_ref, w_down_ref, fg_ref, out_ref, u_ref, *, final):
    tt = x_ref.shape[1]
    first = pl.program_id(1) == 0

    mod = mod_ref[0]
    sh2 = mod[:, 3 * D_MODEL:4 * D_MODEL]
    sc2 = mod[:, 4 * D_MODEL:5 * D_MODEL]
    g2 = mod[:, 5 * D_MODEL:6 * D_MODEL]

    @pl.when(first)
    def _():
        u_ref[0:8, :] = jnp.zeros((8, 2 * D_FF), F32)

    @pl.when(jnp.logical_not(first))
    def _():
        u_ref[0:8, :] = u_ref[tt:tt + 8, :]

    x = x_ref[0]
    h = (_rms(x) * (1.0 + sc2) + sh2).astype(BF16)
    cblk = 512
    for c0 in range(0, 2 * D_FF, cblk):
        u_ref[8:8 + tt, c0:c0 + cblk] = _dot(h, w_up_ref[:, c0:c0 + cblk])

    acc = jnp.zeros((tt, D_MODEL), F32)
    fblk = 256
    for c0 in range(0, D_FF, fblk):
        def conv(col):
            w = cw_ref[:, col:col + fblk]
            return (u_ref[6:6 + tt, col:col + fblk] * w[0:1, :]
                    + u_ref[7:7 + tt, col:col + fblk] * w[1:2, :]
                    + u_ref[8:8 + tt, col:col + fblk] * w[2:3, :]
                    + cb_ref[:, col:col + fblk])
        val = conv(c0)
        gate = conv(D_FF + c0)
        act = (_silu(gate) * val).astype(BF16)
        acc = acc + _dot(act, w_down_ref[c0:c0 + fblk, :])
    y = x + g2 * acc
    if final:
        y = _rms(y) * fg_ref[...]
    out_ref[0] = y


def _ffn(x, mod_l, w_up_b, conv_w, conv_b, w_down_b, final_gain, final):
    bsz, seq, _ = x.shape
    tt = TT_FFN
    return pl.pallas_call(
        functools.partial(_ffn_kernel, final=final),
        grid=(bsz, seq // tt),
        in_specs=[
            pl.BlockSpec((1, tt, D_MODEL), lambda b, t: (b, t, 0)),
            pl.BlockSpec((1, 1, 6 * D_MODEL), lambda b, t: (b, 0, 0)),
            _const_spec(w_up_b.shape),
            _const_spec(conv_w.shape),
            _const_spec(conv_b.shape),
            _const_spec(w_down_b.shape),
            _const_spec(final_gain.shape),
        ],
        out_specs=pl.BlockSpec((1, tt, D_MODEL), lambda b, t: (b, t, 0)),
        out_shape=jax.ShapeDtypeStruct(x.shape, F32),
        scratch_shapes=[pltpu.VMEM((tt + 8, 2 * D_FF), F32)],
        compiler_params=pltpu.CompilerParams(
            dimension_semantics=("arbitrary", "arbitrary"), vmem_limit_bytes=VMEM_LIMIT),
        name="ffn",
    )(x, mod_l, w_up_b, conv_w, conv_b, w_down_b, final_gain)


def kernel(x, c, w_in, w_gla_up, b_gla, lb_logits, head_gain, w_out, w_ada, b_ada,
           w_up, conv_w, conv_b, w_down, final_gain):
    bsz, seq, _ = x.shape
    m3, lvl = _level_constants()
    bd64, bdg, bdr = _block_masks()
    cos, sin = _rotary_tables(seq)
    dmat, qdec, kdec, cdec = _retention_tables()
    consts = (cos, sin, jnp.asarray(m3, BF16), jnp.asarray(lvl, F32),
              jnp.asarray(bd64, BF16), jnp.asarray(bdg, BF16), jnp.asarray(bdr, BF16),
              jnp.asarray(bdg.T, BF16),
              jnp.asarray(bd64, F32), jnp.asarray(bdg, F32), jnp.asarray(bdr, F32),
              dmat, qdec, kdec, cdec)

    mod, lb_all = _prep(c, w_ada, b_ada, lb_logits)
    wg_pad = jnp.zeros((DEPTH, 128, GLA_KW), F32).at[:, :GLA_RANK, :].set(w_gla_up).astype(BF16)
    fg = final_gain.reshape(1, D_MODEL)

    for l in range(DEPTH):
        mod_l = mod[l].reshape(bsz, 1, 6 * D_MODEL)
        x = _mixer(x, mod_l, _pack_w_in(w_in[l]), w_out[l].astype(BF16), wg_pad[l],
                   b_gla[l].reshape(1, GLA_KW), lb_all[l], head_gain[l].reshape(1, D_MODEL), consts)
        x = _ffn(x, mod_l, w_up[l].astype(BF16), conv_w[l], conv_b[l].reshape(1, 2 * D_FF),
                 w_down[l].astype(BF16), fg, final=(l == DEPTH - 1))
    return x
```

```python
import functools

import numpy as np
import jax
import jax.numpy as jnp
from jax import lax
from jax.experimental import pallas as pl
from jax.experimental.pallas import tpu as pltpu

F32 = jnp.float32
BF16 = jnp.bfloat16

D_MODEL = 1024
DEPTH = 2
CHUNK = 64
EPS = 1e-6
HEAD_DV = 64
RET_HEADS, HG_HEADS, GLA_HEADS = 4, 8, 4
RET_W, HG_W, GLA_W = 256, 512, 256
RET_KW, HG_FD, GLA_KW = 256, 512, 128
RET_DK, GLA_DK = 64, 32
GLA_RANK = 16
GLA_GATE_NORM = 16.0
ROPE_BASE = 10000.0
D_FF = 2816
CONV_W = 3

O_RQ, O_RK, O_RV = 0, 256, 512
O_HQ, O_HF, O_HI = 768, 1280, 1792
O_AQ, O_AK, O_AV, O_AL = 2304, 2432, 2560, 2816
PW = 2944
N_IN_PAD = 3968
GW = HG_FD + GLA_KW

SUB_BLOCKS = (16, 4, 1)
N_QV = 3
TOP_SUB_BLOCK = 16
MM_ROWS = (64 + 48 + 32 + 16) + 64 * (4 + 3 + 1)

TT_MIX = 512
TT_FFN = 512
VMEM_LIMIT = 56 * 1024 * 1024


def _dot(a, b):
    return jnp.dot(a, b, preferred_element_type=F32)


def _dot_nt(a, b):
    return lax.dot_general(a, b, (((1,), (1,)), ((), ())), preferred_element_type=F32)


def _dot_tn(a, b):
    return lax.dot_general(a, b, (((0,), (0,)), ((), ())), preferred_element_type=F32)


def _sigmoid(x):
    return 1.0 / (1.0 + jnp.exp(-x))


def _silu(x):
    return x * _sigmoid(x)


def _log_sigmoid(x):
    return jnp.minimum(x, 0.0) - jnp.log1p(jnp.exp(-jnp.abs(x)))


def _rms(x):
    return x * lax.rsqrt(jnp.mean(x * x, axis=-1, keepdims=True) + EPS)


def _split2(x):
    hi = x.astype(BF16)
    lo = (x - hi.astype(F32)).astype(BF16)
    return hi, lo


def _level_constants():
    c = CHUNK
    ii, jj = np.meshgrid(np.arange(c), np.arange(c), indexing="ij")
    mats, masks = [], []
    for s in SUB_BLOCKS:
        sup = 4 * s
        for b in range(N_QV):
            mk = ((ii // sup) == (jj // sup)) & (((jj % sup) // s) == b) & (((ii % sup) // s) > b)
            masks.append(np.tile(mk.astype(np.float32), (1, 4)))
        if s > 1:
            m = np.zeros((c, c), np.float32)
            for j in range(c):
                b = (j % sup) // s
                if b < N_QV:
                    end = (j // sup) * sup + (b + 1) * s - 1
                    m[j, j + 1:end + 1] = 1.0
            mats.append(m)
        for b in range(N_QV):
            m = np.zeros((c, c), np.float32)
            for i in range(c):
                if (i % sup) // s > b:
                    end = (i // sup) * sup + (b + 1) * s - 1
                    m[i, end + 1:i + 1] = 1.0
            mats.append(m[(b + 1) * s:] if s == TOP_SUB_BLOCK else m)
    mats.append(np.tril(np.ones((c, c), np.float32)))
    mstack = np.concatenate(mats, axis=0)
    assert mstack.shape[0] == MM_ROWS
    m3 = np.tile(mstack, (1, 2))
    return m3, np.stack(masks)


def _block_masks():
    r = np.arange(256)[:, None] // 64
    bd64 = (r == (np.arange(256)[None, :] // 64)).astype(np.float32)
    bdg = (r == (np.arange(128)[None, :] // 32)).astype(np.float32)
    bdr = (r == ((np.arange(256)[None, :] % 128) // 32)).astype(np.float32)
    return bd64, bdg, bdr


def _retention_tables():
    h = jnp.arange(RET_HEADS, dtype=F32)
    lg = jnp.log1p(-(2.0 ** (-5.0 - h)))
    idx = jnp.arange(CHUNK, dtype=F32)
    rel = idx[:, None] - idx[None, :]
    decay = jnp.where(rel >= 0, jnp.exp(lg[:, None, None] * jnp.maximum(rel, 0.0)), 0.0)
    dmat = jnp.transpose(decay, (1, 0, 2)).reshape(CHUNK, RET_HEADS * CHUNK)
    lane_head = (jnp.arange(256) % 128) // 32
    lg_lane = lg[lane_head]
    qdec = jnp.exp(lg_lane[None, :] * (idx[:, None] + 1.0))
    kdec = jnp.exp(lg_lane[None, :] * (CHUNK - 1 - idx[:, None]))
    cdec = jnp.exp(lg_lane * CHUNK)[None, :]
    return dmat, qdec, kdec, cdec


def _rotary_tables(seq):
    half = RET_DK // 2
    inv = ROPE_BASE ** (-jnp.arange(0, RET_DK, 2, dtype=F32) / RET_DK)
    ang = jnp.arange(seq, dtype=jnp.int32).astype(F32)[:, None] * inv[None, :]
    cos = jnp.tile(jnp.cos(ang), (1, 128 // half))
    sin = jnp.tile(jnp.sin(ang), (1, 128 // half))
    return cos, sin


def _pack_w_in(w):
    def rot_perm(block):
        b = block.reshape(D_MODEL, RET_HEADS, 2, RET_DK // 2)
        return jnp.transpose(b, (0, 2, 1, 3)).reshape(D_MODEL, RET_KW)
    rq = rot_perm(w[:, 0:256])
    rk = rot_perm(w[:, 256:512])
    rv, rg = w[:, 512:768], w[:, 768:1024]
    hq_hf_hi, hg = w[:, 1024:2560], w[:, 2560:3072]
    aq_ak_av, ag, al = w[:, 3072:3584], w[:, 3584:3840], w[:, 3840:3856]
    pad = jnp.zeros((D_MODEL, 128 - GLA_RANK), w.dtype)
    return jnp.concatenate([rq, rk, rv, hq_hf_hi, aq_ak_av, al, pad, rg, hg, ag], axis=1).astype(BF16)


def _prep_kernel(c_ref, w_ref, b_ref, lbl_ref, mod_ref, lb_ref):
    ca = _silu(c_ref[...])
    w = w_ref[0]
    c_hi, c_lo = _split2(ca)
    w_hi, w_lo = _split2(w)
    acc = _dot(c_hi, w_hi) + _dot(c_lo, w_hi) + _dot(c_hi, w_lo)
    mod_ref[0] = acc + b_ref[0]
    rows = [lbl_ref[l] for l in range(DEPTH)]
    mx = functools.reduce(jnp.maximum, rows)
    ex = [jnp.exp(r - mx) for r in rows]
    den = functools.reduce(lambda a, b: a + b, ex)
    run = None
    first = None
    for l in range(DEPTH):
        sm = ex[l] / den
        run = sm if run is None else run + sm
        if first is None:
            first = run
        lb_ref[l] = run - first


def _prep(c, w_ada, b_ada, lb_logits):
    b = c.shape[0]
    nblk = 6 * D_MODEL // 1024
    return pl.pallas_call(
        _prep_kernel,
        grid=(DEPTH, nblk),
        in_specs=[
            pl.BlockSpec((b, D_MODEL), lambda l, j: (0, 0)),
            pl.BlockSpec((1, D_MODEL, 1024), lambda l, j: (l, 0, j)),
            pl.BlockSpec((1, 1, 1024), lambda l, j: (l, 0, j)),
            pl.BlockSpec((DEPTH, 1, HG_FD), lambda l, j: (0, 0, 0)),
        ],
        out_specs=[
            pl.BlockSpec((1, b, 1024), lambda l, j: (l, 0, j)),
            pl.BlockSpec((DEPTH, 1, HG_FD), lambda l, j: (0, 0, 0)),
        ],
        out_shape=[
            jax.ShapeDtypeStruct((DEPTH, b, 6 * D_MODEL), F32),
            jax.ShapeDtypeStruct((DEPTH, 1, HG_FD), F32),
        ],
        compiler_params=pltpu.CompilerParams(dimension_semantics=("arbitrary", "arbitrary")),
        name="prep",
    )(c, w_ada, b_ada.reshape(DEPTH, 1, 6 * D_MODEL), lb_logits.reshape(DEPTH, 1, HG_FD))


def _mixer_kernel(x_ref, mod_ref, w_in_ref, w_out_ref, wg_ref, bg_ref, lb_ref, gain_ref,
                  cos_ref, sin_ref, m3_ref, lvl_ref, bd64b_ref, bdgb_ref, bdrb_ref, bdgt_ref,
                  bd64f_ref, bdgf_ref, bdrf_ref, dmat_ref, qdec_ref, kdec_ref, cdec_ref,
                  out_ref,
                  proj_ref, sg_ref, lf_ref, qg_ref, kg_ref, qr_ref, kr_ref, o_ref,
                  st_r_ref, st_h0_ref, st_h1_ref, st_g_ref):
    tt = x_ref.shape[1]

    @pl.when(pl.program_id(1) == 0)
    def _():
        st_r_ref[...] = jnp.zeros_like(st_r_ref)
        st_h0_ref[...] = jnp.zeros_like(st_h0_ref)
        st_h1_ref[...] = jnp.zeros_like(st_h1_ref)
        st_g_ref[...] = jnp.zeros_like(st_g_ref)

    mod = mod_ref[0]
    sh1 = mod[:, 0:D_MODEL]
    sc1 = mod[:, D_MODEL:2 * D_MODEL]
    g1 = mod[:, 2 * D_MODEL:3 * D_MODEL]
    gain = gain_ref[...]

    h = (_rms(x_ref[0]) * (1.0 + sc1) + sh1).astype(BF16)

    def inproj(c0, c1):
        return _dot(h, w_in_ref[:, c0:c1])

    def store_log_gates(cols, lf):
        lf_ref[:, cols] = lf

    for c0 in range(PW, N_IN_PAD, 512):
        cols = slice(c0 - PW, c0 - PW + 512)
        sg_ref[:, cols] = _silu(inproj(c0, c0 + 512)) * gain[:, cols]

    lb = lb_ref[0]
    qg_ref[:, 0:HG_FD] = _silu(inproj(O_HQ, O_HQ + HG_FD))
    z = inproj(O_HF, O_HF + HG_FD)
    a = jnp.log1p(-lb) + _log_sigmoid(z)
    lb_pos = lb > 0.0
    b = jnp.log(jnp.where(lb_pos, lb, 1.0))
    both = jnp.maximum(a, b) + jnp.log1p(jnp.exp(-jnp.abs(a - b)))
    store_log_gates(slice(0, HG_FD), jnp.where(lb_pos, both, a))
    kg_ref[:, 0:HG_FD] = (1.0 - lb) * _sigmoid(-z)

    rqk = inproj(O_RQ, O_RQ + 512)
    cos = cos_ref[...]
    sin = sin_ref[...]
    q1, q2, k1, k2 = rqk[:, 0:128], rqk[:, 128:256], rqk[:, 256:384], rqk[:, 384:512]
    kscale = RET_DK ** -0.5
    qr_ref[:, 0:128] = q1 * cos - q2 * sin
    qr_ref[:, 128:256] = q1 * sin + q2 * cos
    kr_ref[:, 0:128] = (k1 * cos - k2 * sin) * kscale
    kr_ref[:, 128:256] = (k1 * sin + k2 * cos) * kscale

    aqk = inproj(O_AQ, O_AQ + 256)
    qg_ref[:, HG_FD:GW] = aqk[:, 0:GLA_KW]
    kg_ref[:, HG_FD:GW] = aqk[:, GLA_KW:2 * GLA_KW] * (GLA_DK ** -0.5)
    alow = inproj(O_AL, O_AL + 128).astype(BF16)
    glogit = _dot(alow, wg_ref[...]) + bg_ref[...]
    store_log_gates(slice(HG_FD, GW), _log_sigmoid(glogit) * (1.0 / GLA_GATE_NORM))

    proj_ref[:, O_RV:O_RV + RET_W] = inproj(O_RV, O_RV + RET_W)
    proj_ref[:, O_HI:O_HI + HG_W] = inproj(O_HI, O_HI + HG_W)
    proj_ref[:, O_AV:O_AV + GLA_W] = inproj(O_AV, O_AV + GLA_W)

    bd64b = bd64b_ref[...]
    bdgb = bdgb_ref[...]
    bdrb = bdrb_ref[...]
    bdgt = bdgt_ref[...]

    groups = (
        (0, 256, O_HI, 256, st_h0_ref, bd64b, bd64b, bd64f_ref),
        (256, 512, O_HI + 256, 512, st_h1_ref, bd64b, bd64b, bd64f_ref),
        (512, 640, O_AV, 768, st_g_ref, bdgb, bdgt, bdgf_ref),
    )

    def chunk_body(ci, carry):
        r0 = pl.multiple_of(ci * CHUNK, CHUNK)
        rows = pl.ds(r0, CHUNK)

        qr = qr_ref[rows, :]
        kr = kr_ref[rows, :]
        vr = proj_ref[rows, O_RV:O_RV + RET_W].astype(BF16)
        qrb = qr.astype(BF16)
        krb = kr.astype(BF16)
        kst = jnp.concatenate([krb] * 4, axis=0) * bdrb
        p = _dot_nt(qrb, kst) * dmat_ref[...]
        vst = jnp.concatenate([vr] * 4, axis=0) * bd64b
        o_r = _dot(p.astype(BF16), vst)
        st = st_r_ref[...]
        o_r = o_r + _dot_nt((qr * qdec_ref[...]).astype(BF16), st.astype(BF16))
        kv = _dot_tn(vr, (kr * kdec_ref[...]).astype(BF16))
        st_r_ref[...] = st * cdec_ref[...] + kv * bdrf_ref[...]
        o_ref[rows, 0:RET_W] = o_r

        qg = qg_ref[rows, :]
        kg = kg_ref[rows, :]
        hi, lo = _split2(lf_ref[rows, :])
        gmat = _dot(m3_ref[...], jnp.concatenate([hi, lo], axis=0))
        acc = [jnp.zeros((CHUNK, 256), F32) for _ in groups]
        cum = gmat[MM_ROWS - CHUNK:MM_ROWS, :]
        off = 0
        for li, s in enumerate(SUB_BLOCKS):
            if s > 1:
                zk = (kg * jnp.exp(gmat[off:off + CHUNK, :])).astype(BF16)
                off += CHUNK
            else:
                zk = kg.astype(BF16)
            first = [(b + 1) * s if s == TOP_SUB_BLOCK else 0 for b in range(N_QV)]
            zq, starts = [], []
            for b in range(N_QV):
                n = CHUNK - first[b]
                zq.append((qg[first[b]:, :] * jnp.exp(gmat[off:off + n, :])).astype(BF16))
                starts.append(sum(CHUNK - f for f in first[:b]))
                off += n
            zq = jnp.concatenate(zq, axis=0)
            for gi, (l0, l1, _, _, _, bdk, _, _) in enumerate(groups):
                kst = jnp.concatenate([zk[:, l0:l1]] * 4, axis=0) * bdk
                sc = _dot_nt(zq[:, l0:l1], kst)
                for b in range(N_QV):
                    n = CHUNK - first[b]
                    part = sc[starts[b]:starts[b] + n, :] * lvl_ref[N_QV * li + b][first[b]:, :]
                    if first[b]:
                        part = jnp.concatenate([jnp.zeros((first[b], 256), F32), part], axis=0)
                    acc[gi] = acc[gi] + part
        last = cum[CHUNK - 1:CHUNK, :]
        qc = (qg * jnp.exp(cum)).astype(BF16)
        kc = (kg * jnp.exp(last - cum)).astype(BF16)
        elast = jnp.exp(last)
        qk = (qg * kg).astype(BF16)
        for gi, (l0, l1, vcol, ocol, st_ref, _, ones, bdf_ref) in enumerate(groups):
            v = proj_ref[rows, vcol:vcol + 256]
            vb = v.astype(BF16)
            vst = jnp.concatenate([vb] * 4, axis=0) * bd64b
            o_g = _dot(acc[gi].astype(BF16), vst)
            o_g = o_g + _dot(qk[:, l0:l1], ones) * v
            st = st_ref[...]
            o_g = o_g + _dot_nt(qc[:, l0:l1], st.astype(BF16))
            kv = _dot_tn(vb, kc[:, l0:l1])
            st_ref[...] = st * elast[:, l0:l1] + kv * bdf_ref[...]
            o_ref[rows, ocol:ocol + 256] = o_g
        return carry

    lax.fori_loop(0, tt // CHUNK, chunk_body, 0, unroll=True)

    ys = []
    for bi in range(4):
        cols = slice(bi * 256, (bi + 1) * 256)
        o = o_ref[:, cols]
        ms = _dot((o * o).astype(BF16), bd64b) * (1.0 / HEAD_DV)
        ys.append((o * lax.rsqrt(ms + EPS) * sg_ref[:, cols]).astype(BF16))
    y = jnp.concatenate(ys, axis=1)
    out_ref[0] = x_ref[0] + g1 * _dot(y, w_out_ref[...])


def _const_spec(shape):
    nd = len(shape)
    return pl.BlockSpec(shape, lambda b, t, _nd=nd: (0,) * _nd, pipeline_mode=pl.Buffered(1))


def _mixer(x, mod_l, w_in_p, w_out_b, wg, bg, lb_l, gain, consts):
    bsz, seq, _ = x.shape
    tt = TT_MIX
    (cos, sin, m3, lvl, bd64b, bdgb, bdrb, bdgt, bd64f, bdgf, bdrf, dmat, qdec, kdec, cdec) = consts
    const_inputs = (m3, lvl, bd64b, bdgb, bdrb, bdgt, bd64f, bdgf, bdrf, dmat, qdec, kdec, cdec)
    in_specs = [
        pl.BlockSpec((1, tt, D_MODEL), lambda b, t: (b, t, 0)),
        pl.BlockSpec((1, 1, 6 * D_MODEL), lambda b, t: (b, 0, 0)),
        _const_spec(w_in_p.shape),
        _const_spec(w_out_b.shape),
        _const_spec(wg.shape),
        _const_spec(bg.shape),
        _const_spec(lb_l.shape),
        _const_spec(gain.shape),
        pl.BlockSpec((tt, 128), lambda b, t: (t, 0)),
        pl.BlockSpec((tt, 128), lambda b, t: (t, 0)),
    ] + [_const_spec(a.shape) for a in const_inputs]
    scratch = [
        pltpu.VMEM((tt, PW), F32),
        pltpu.VMEM((tt, D_MODEL), F32),
        pltpu.VMEM((tt, GW), F32),
        pltpu.VMEM((tt, GW), F32),
        pltpu.VMEM((tt, GW), F32),
        pltpu.VMEM((tt, RET_KW), F32),
        pltpu.VMEM((tt, RET_KW), F32),
        pltpu.VMEM((tt, D_MODEL), F32),
        pltpu.VMEM((256, 256), F32),
        pltpu.VMEM((256, 256), F32),
        pltpu.VMEM((256, 256), F32),
        pltpu.VMEM((256, 128), F32),
    ]
    return pl.pallas_call(
        _mixer_kernel,
        grid=(bsz, seq // tt),
        in_specs=in_specs,
        out_specs=pl.BlockSpec((1, tt, D_MODEL), lambda b, t: (b, t, 0)),
        out_shape=jax.ShapeDtypeStruct(x.shape, F32),
        scratch_shapes=scratch,
        compiler_params=pltpu.CompilerParams(
            dimension_semantics=("arbitrary", "arbitrary"), vmem_limit_bytes=VMEM_LIMIT),
        name="mixer",
    )(x, mod_l, w_in_p, w_out_b, wg, bg, lb_l, gain, cos, sin, *const_inputs)


def _ffn_kernel(x_ref, mod_ref, w_up_ref, cw_ref, cb_ref, w_down_ref, fg_ref, out_ref, u_ref, *, final):
    tt = x_ref.shape[1]
    first = pl.program_id(1) == 0

    mod = mod_ref[0]
    sh2 = mod[:, 3 * D_MODEL:4 * D_MODEL]
    sc2 = mod[:, 4 * D_MODEL:5 * D_MODEL]
    g2 = mod[:, 5 * D_MODEL:6 * D_MODEL]

    @pl.when(first)
    def _():
        u_ref[0:8, :] = jnp.zeros((8, 2 * D_FF), F32)

    @pl.when(jnp.logical_not(first))
    def _():
        u_ref[0:8, :] = u_ref[tt:tt + 8, :]

    x = x_ref[0]
    h = (_rms(x) * (1.0 + sc2) + sh2).astype(BF16)
    cblk = 512
    for c0 in range(0, 2 * D_FF, cblk):
        u_ref[8:8 + tt, c0:c0 + cblk] = _dot(h, w_up_ref[:, c0:c0 + cblk])

    acc = jnp.zeros((tt, D_MODEL), F32)
    fblk = 256
    for c0 in range(0, D_FF, fblk):
        def conv(col):
            w = cw_ref[:, col:col + fblk]
            return (u_ref[6:6 + tt, col:col + fblk] * w[0:1, :]
                    + u_ref[7:7 + tt, col:col + fblk] * w[1:2, :]
                    + u_ref[8:8 + tt, col:col + fblk] * w[2:3, :]
                    + cb_ref[:, col:col + fblk])
        val = conv(c0)
        gate = conv(D_FF + c0)
        act = (_silu(gate) * val).astype(BF16)
        acc = acc + _dot(act, w_down_ref[c0:c0 + fblk, :])
    y = x + g2 * acc
    if final:
        y = _rms(y) * fg_ref[...]
    out_ref[0] = y


def _ffn(x, mod_l, w_up_b, conv_w, conv_b, w_down_b, final_gain, final):
    bsz, seq, _ = x.shape
    tt = TT_FFN
    return pl.pallas_call(
        functools.partial(_ffn_kernel, final=final),
        grid=(bsz, seq // tt),
        in_specs=[
            pl.BlockSpec((1, tt, D_MODEL), lambda b, t: (b, t, 0)),
            pl.BlockSpec((1, 1, 6 * D_MODEL), lambda b, t: (b, 0, 0)),
            _const_spec(w_up_b.shape),
            _const_spec(conv_w.shape),
            _const_spec(conv_b.shape),
            _const_spec(w_down_b.shape),
            _const_spec(final_gain.shape),
        ],
        out_specs=pl.BlockSpec((1, tt, D_MODEL), lambda b, t: (b, t, 0)),
        out_shape=jax.ShapeDtypeStruct(x.shape, F32),
        scratch_shapes=[pltpu.VMEM((tt + 8, 2 * D_FF), F32)],
        compiler_params=pltpu.CompilerParams(
            dimension_semantics=("arbitrary", "arbitrary"), vmem_limit_bytes=VMEM_LIMIT),
        name="ffn",
    )(x, mod_l, w_up_b, conv_w, conv_b, w_down_b, final_gain)


def kernel(x, c, w_in, w_gla_up, b_gla, lb_logits, head_gain, w_out, w_ada, b_ada,
           w_up, conv_w, conv_b, w_down, final_gain):
    bsz, seq, _ = x.shape
    m3, lvl = _level_constants()
    bd64, bdg, bdr = _block_masks()
    cos, sin = _rotary_tables(seq)
    dmat, qdec, kdec, cdec = _retention_tables()
    consts = (cos, sin, jnp.asarray(m3, BF16), jnp.asarray(lvl, F32),
              jnp.asarray(bd64, BF16), jnp.asarray(bdg, BF16), jnp.asarray(bdr, BF16),
              jnp.asarray(bdg.T, BF16),
              jnp.asarray(bd64, F32), jnp.asarray(bdg, F32), jnp.asarray(bdr, F32),
              dmat, qdec, kdec, cdec)

    mod, lb_all = _prep(c, w_ada, b_ada, lb_logits)
    wg_pad = jnp.zeros((DEPTH, 128, GLA_KW), F32).at[:, :GLA_RANK, :].set(w_gla_up).astype(BF16)
    fg = final_gain.reshape(1, D_MODEL)

    for l in range(DEPTH):
        mod_l = mod[l].reshape(bsz, 1, 6 * D_MODEL)
        x = _mixer(x, mod_l, _pack_w_in(w_in[l]), w_out[l].astype(BF16), wg_pad[l],
                   b_gla[l].reshape(1, GLA_KW), lb_all[l], head_gain[l].reshape(1, D_MODEL), consts)
        x = _ffn(x, mod_l, w_up[l].astype(BF16), conv_w[l], conv_b[l].reshape(1, 2 * D_FF),
                 w_down[l].astype(BF16), fg, final=(l == DEPTH - 1))
    return x
```
